```python
import math
import jax, jax.numpy as jnp
from jax import lax
import numpy as np

D_MODEL = 1024
BATCH = 2
SEQ = 8192
DEPTH = 1

CHUNK = 64
Q_BLOCK = 128
RMS_EPS = 1e-6
D_FF = 4 * D_MODEL

RW_HEADS = 8
RW_HEAD_DIM = 64
RW_WIDTH = RW_HEADS * RW_HEAD_DIM
RW_DECAY_LORA = 64
RW_ICLR_LORA = 64
RW_GATE_LORA = 128
RW_GN_EPS = 1e-5 * RW_HEAD_DIM
RW_IN = 3 * RW_WIDTH + RW_DECAY_LORA + RW_ICLR_LORA + RW_GATE_LORA

DA_HEADS = 4
DA_QK_DIM = 64
DA_V_DIM = 2 * DA_QK_DIM
DA_QK_WIDTH = DA_HEADS * 2 * DA_QK_DIM
DA_V_WIDTH = DA_HEADS * DA_V_DIM
DA_IN = 2 * DA_QK_WIDTH + DA_V_WIDTH
DA_SUBLN_EPS = 1e-5

GATE_IN = 2 * D_MODEL
IN_WIDTH = RW_IN + DA_IN + GATE_IN

kernel_name = "hybrid_rwkv7_diffattn_gated_block"


def rms_norm(x, g, eps=RMS_EPS):
    xf = x.astype(jnp.float32)
    y = xf * lax.rsqrt(jnp.mean(xf * xf, axis=-1, keepdims=True) + eps)
    return (y * g.astype(jnp.float32)).astype(x.dtype)


def token_shift(u):
    return jnp.pad(u, ((0, 0), (1, 0), (0, 0)))[:, :-1]


def wkv7_scan(r, decay, k, v, kk, a):
    b, s, h, d = r.shape
    seq_first = lambda t: jnp.transpose(t, (1, 0, 2, 3))

    def step(state, inp):
        r_t, w_t, k_t, v_t, kk_t, a_t = inp
        sa = jnp.einsum('bhij,bhj->bhi', state, -kk_t)
        state = (state * w_t[:, :, None, :]
                 + sa[..., None] * (kk_t * a_t)[:, :, None, :]
                 + v_t[..., None] * k_t[:, :, None, :])
        y = jnp.einsum('bhij,bhj->bhi', state, r_t)
        return state, y

    state0 = jnp.zeros((b, h, d, d), jnp.float32)
    _, ys = lax.scan(step, state0, tuple(seq_first(t) for t in (r, decay, k, v, kk, a)))
    return jnp.transpose(ys, (1, 0, 2, 3))


def rwkv7_time_mix(u, mu, w0, w2, a0, a2, g2, k_k, k_a, r_k, gn_g, gn_b):
    b, s, _ = u.shape
    u = u + (token_shift(u) - u) * mu
    o1 = RW_WIDTH; o2 = 2 * RW_WIDTH; o3 = 3 * RW_WIDTH
    o4 = o3 + RW_DECAY_LORA; o5 = o4 + RW_ICLR_LORA
    r, k, v = u[..., :o1], u[..., o1:o2], u[..., o2:o3]
    xw, xa, xg = u[..., o3:o4], u[..., o4:o5], u[..., o5:]
    f32 = jnp.float32
    w_log = -jax.nn.softplus(-(w0 + jnp.tanh(xw) @ w2).astype(f32)) - 0.5
    decay = jnp.exp(-jnp.exp(w_log))
    a = jax.nn.sigmoid((a0 + xa @ a2).astype(f32))
    g = jax.nn.sigmoid(xg) @ g2
    heads = lambda t: t.astype(f32).reshape(b, s, RW_HEADS, RW_HEAD_DIM)
    r_h, k_h, v_h, a_h, w_h = heads(r), heads(k), heads(v), heads(a), heads(decay)
    kk = k_h * heads(k_k[None, None, :] * jnp.ones((b, s, 1), f32))
    kk = kk / jnp.maximum(jnp.linalg.norm(kk, axis=-1, keepdims=True), 1e-12)
    k_h = k_h * (1.0 + (a_h - 1.0) * k_a.astype(f32).reshape(RW_HEADS, RW_HEAD_DIM))
    y = wkv7_scan(r_h, w_h, k_h, v_h, kk, a_h)
    mean = jnp.mean(y, axis=-1, keepdims=True)
    var = jnp.mean(jnp.square(y - mean), axis=-1, keepdims=True)
    y = ((y - mean) * lax.rsqrt(var + RW_GN_EPS)).reshape(b, s, RW_WIDTH)
    y = y * gn_g.astype(f32) + gn_b.astype(f32)
    bonus = jnp.sum(r_h * k_h * r_k.astype(f32), axis=-1, keepdims=True) * v_h
    out = (y + bonus.reshape(b, s, RW_WIDTH)) * g.astype(f32)
    return out.astype(u.dtype)


def diff_attention(u, lam_q1, lam_k1, lam_q2, lam_k2, sub_g, lambda_init):
    b, s, _ = u.shape
    q = u[..., :DA_QK_WIDTH].reshape(b, s, DA_HEADS, 2, DA_QK_DIM)
    k = u[..., DA_QK_WIDTH:2 * DA_QK_WIDTH].reshape(b, s, DA_HEADS, 2, DA_QK_DIM)
    v = u[..., 2 * DA_QK_WIDTH:].reshape(b, s, DA_HEADS, DA_V_DIM)
    f32 = jnp.float32
    lam = (jnp.exp(jnp.sum(lam_q1.astype(f32) * lam_k1.astype(f32)))
           - jnp.exp(jnp.sum(lam_q2.astype(f32) * lam_k2.astype(f32))) + lambda_init)
    scale = DA_QK_DIM ** -0.5
    n_blocks = s // Q_BLOCK
    q_blocks = jnp.transpose(q.reshape(b, n_blocks, Q_BLOCK, DA_HEADS, 2, DA_QK_DIM),
                             (1, 0, 2, 3, 4, 5))
    key_chunk = jnp.arange(s, dtype=jnp.int32) // CHUNK

    def block(args):
        qb, bi = args
        q_chunk = (bi * Q_BLOCK + jnp.arange(Q_BLOCK, dtype=jnp.int32)) // CHUNK
        mask = key_chunk[None, :] <= q_chunk[:, None]
        sc = jnp.einsum('bqhmd,bkhmd->bhmqk', qb, k).astype(f32) * scale
        sc = jnp.where(mask, sc, -jnp.inf)
        p = jax.nn.softmax(sc, axis=-1)
        attn = p[:, :, 0] - lam * p[:, :, 1]
        return jnp.einsum('bhqk,bkhd->bqhd', attn.astype(v.dtype), v)

    o = lax.map(block, (q_blocks, jnp.arange(n_blocks, dtype=jnp.int32)))
    o = jnp.transpose(o, (1, 0, 2, 3, 4)).reshape(b, s, DA_HEADS, DA_V_DIM)
    o = rms_norm(o, sub_g, DA_SUBLN_EPS) * (1.0 - lambda_init)
    return o.reshape(b, s, DA_V_WIDTH)


def setup_inputs(seed: int = 0) -> dict:
    key = jax.random.key(seed)
    ks = iter(jax.random.split(key, 40))
    nrm = lambda shape, scale: jax.random.normal(next(ks), shape, jnp.float32) * scale
    L = DEPTH
    gain = lambda n: 1.0 + nrm((L, n), 0.02)
    return {
        "x": nrm((BATCH, SEQ, D_MODEL), 1.0),
        "pre_mix_g": gain(D_MODEL),
        "w_in": nrm((L, D_MODEL, IN_WIDTH), D_MODEL ** -0.5),
        "rw_shift_mu": jax.random.uniform(next(ks), (L, RW_IN), jnp.float32),
        "rw_w0": 0.5 + nrm((L, RW_WIDTH), 0.5),
        "rw_w2": nrm((L, RW_DECAY_LORA, RW_WIDTH), 0.1),
        "rw_a0": nrm((L, RW_WIDTH), 0.5),
        "rw_a2": nrm((L, RW_ICLR_LORA, RW_WIDTH), 0.1),
        "rw_g2": nrm((L, RW_GATE_LORA, RW_WIDTH), RW_GATE_LORA ** -0.5),
        "rw_k_k": 0.85 + nrm((L, RW_WIDTH), 0.05),
        "rw_k_a": 1.0 + nrm((L, RW_WIDTH), 0.05),
        "rw_r_k": nrm((L, RW_HEADS, RW_HEAD_DIM), 0.1),
        "rw_gn_g": gain(RW_WIDTH),
        "rw_gn_b": nrm((L, RW_WIDTH), 0.02),
        "w_rw_out": nrm((L, RW_WIDTH, D_MODEL), RW_WIDTH ** -0.5),
        "da_lam_q1": nrm((L, DA_QK_DIM), 0.1),
        "da_lam_k1": nrm((L, DA_QK_DIM), 0.1),
        "da_lam_q2": nrm((L, DA_QK_DIM), 0.1),
        "da_lam_k2": nrm((L, DA_QK_DIM), 0.1),
        "da_sub_g": gain(DA_V_DIM),
        "w_da_out": nrm((L, DA_V_WIDTH, D_MODEL), DA_V_WIDTH ** -0.5),
        "w_o": nrm((L, D_MODEL, D_MODEL), D_MODEL ** -0.5),
        "post_mix_g": gain(D_MODEL),
        "pre_mlp_g": gain(D_MODEL),
        "w_up": nrm((L, D_MODEL, D_FF), D_MODEL ** -0.5),
        "w_down": nrm((L, D_FF, D_MODEL), D_FF ** -0.5),
        "post_mlp_g": gain(D_MODEL),
    }


def reference(x, pre_mix_g, w_in, rw_shift_mu, rw_w0, rw_w2, rw_a0, rw_a2, rw_g2, rw_k_k,
              rw_k_a, rw_r_k, rw_gn_g, rw_gn_b, w_rw_out, da_lam_q1, da_lam_k1, da_lam_q2,
              da_lam_k2, da_sub_g, w_da_out, w_o, post_mix_g, pre_mlp_g, w_up, w_down,
              post_mlp_g):
    for l in range(DEPTH):
        lambda_init = 0.8 - 0.6 * math.exp(-0.3 * l)
        h = rms_norm(x, pre_mix_g[l])
        z = h @ w_in[l]
        z_rw = z[..., :RW_IN]
        z_da = z[..., RW_IN:RW_IN + DA_IN]
        gate_rw = z[..., RW_IN + DA_IN:RW_IN + DA_IN + D_MODEL]
        gate_da = z[..., RW_IN + DA_IN + D_MODEL:]
        y_rw = rwkv7_time_mix(z_rw, rw_shift_mu[l], rw_w0[l], rw_w2[l], rw_a0[l], rw_a2[l],
                              rw_g2[l], rw_k_k[l], rw_k_a[l], rw_r_k[l], rw_gn_g[l],
                              rw_gn_b[l]) @ w_rw_out[l]
        y_da = diff_attention(z_da, da_lam_q1[l], da_lam_k1[l], da_lam_q2[l], da_lam_k2[l],
                              da_sub_g[l], lambda_init) @ w_da_out[l]
        merged = jax.nn.sigmoid(gate_rw) * y_rw + jax.nn.sigmoid(gate_da) * y_da
        x = x + rms_norm(merged @ w_o[l], post_mix_g[l])
        h = rms_norm(x, pre_mlp_g[l])
        f = jnp.square(jax.nn.relu(h @ w_up[l])) @ w_down[l]
        x = x + rms_norm(f, post_mlp_g[l])
    return x
```

```python
import functools
import math

import jax
import jax.numpy as jnp
from jax import lax
from jax.experimental import pallas as pl
from jax.experimental.pallas import tpu as pltpu

F32 = jnp.float32
BF16 = jnp.bfloat16

RMS_EPS = 1e-6
CHUNK = 64
RW_HEADS = 8
RW_HEAD_DIM = 64
RW_WIDTH = RW_HEADS * RW_HEAD_DIM
RW_DECAY_LORA = 64
RW_ICLR_LORA = 64
RW_GATE_LORA = 128
RW_GN_EPS = 1e-5 * RW_HEAD_DIM
RW_IN = 3 * RW_WIDTH + RW_DECAY_LORA + RW_ICLR_LORA + RW_GATE_LORA
DA_HEADS = 4
DA_QK_DIM = 64
DA_V_DIM = 2 * DA_QK_DIM
DA_QK_WIDTH = DA_HEADS * 2 * DA_QK_DIM
DA_V_WIDTH = DA_HEADS * DA_V_DIM
DA_IN = 2 * DA_QK_WIDTH + DA_V_WIDTH
DA_SUBLN_EPS = 1e-5

LANES = 128
WKV_CHUNK = 64
VMEM_LIMIT = 56 * 1024 * 1024

HIGHEST = lax.Precision.HIGHEST


def _cparams(semantics):
    return pltpu.CompilerParams(dimension_semantics=semantics, vmem_limit_bytes=VMEM_LIMIT)


def _rms(x, g, eps):
    return x * lax.rsqrt(jnp.mean(x * x, axis=-1, keepdims=True) + eps) * g


def _dot(a, b):
    return jnp.dot(a, b, preferred_element_type=F32)


def _dot_nt(a, b):
    return lax.dot_general(a, b, (((1,), (1,)), ((), ())), preferred_element_type=F32)


def _dot_tn(a, b):
    return lax.dot_general(a, b, (((0,), (0,)), ((), ())), preferred_element_type=F32)


def _inproj_kernel(x_ref, g_ref, w_ref, o_ref):
    h = _rms(x_ref[...], g_ref[...], RMS_EPS)
    o_ref[...] = _dot(h.astype(BF16), w_ref[...]).astype(o_ref.dtype)


def _inproj(x2d, g, w_bf16, out_dtype, tm, tn):
    t, d = x2d.shape
    n = w_bf16.shape[1]
    return pl.pallas_call(
        _inproj_kernel,
        grid=(n // tn, t // tm),
        in_specs=[
            pl.BlockSpec((tm, d), lambda j, i: (i, 0)),
            pl.BlockSpec((1, d), lambda j, i: (0, 0)),
            pl.BlockSpec((d, tn), lambda j, i: (0, j)),
        ],
        out_specs=pl.BlockSpec((tm, tn), lambda j, i: (i, j)),
        out_shape=jax.ShapeDtypeStruct((t, n), out_dtype),
        compiler_params=_cparams(("parallel", "parallel")),
        name="inproj",
    )(x2d, g, w_bf16)


def _sigmoid(x):
    return 1.0 / (1.0 + jnp.exp(-x))


def _wkv_kernel(r_ref, k_ref, v_ref, lo_ref, mur_ref, muk_ref, muv_ref, mulo_ref,
                w0_ref, a0_ref, kk_ref, ka_ref, rk_ref, gng_ref, gnb_ref,
                w2_ref, a2_ref, g2_ref, o_ref,
                p_ref, cr_ref, ck_ref, cv_ref, clo_ref,
                sr_ref, sk_ref, sv_ref, sa_ref, sb_ref, sw_ref, sy_ref):
    ts = r_ref.shape[1]
    nc = ts // WKV_CHUNK
    c = WKV_CHUNK

    @pl.when(pl.program_id(2) == 0)
    def _():
        p_ref[...] = jnp.zeros_like(p_ref)
        cr_ref[...] = jnp.zeros_like(cr_ref)
        ck_ref[...] = jnp.zeros_like(ck_ref)
        cv_ref[...] = jnp.zeros_like(cv_ref)
        clo_ref[...] = jnp.zeros_like(clo_ref)

    def shifted(z, carry_ref, mu):
        prev = pltpu.roll(z, 1, axis=0)
        row = lax.broadcasted_iota(jnp.int32, z.shape, 0)
        prev = jnp.where(row == 0, carry_ref[...], prev)
        carry_ref[...] = z[ts - 1:ts, :]
        return z + (prev - z) * mu

    r = shifted(r_ref[0], cr_ref, mur_ref[...])
    k = shifted(k_ref[0], ck_ref, muk_ref[...])
    v = shifted(v_ref[0], cv_ref, muv_ref[...])
    lo = shifted(lo_ref[0], clo_ref, mulo_ref[...])
    xwa = lo[:, :LANES]
    xg = lo[:, LANES:]

    zeros64 = jnp.zeros((RW_DECAY_LORA, LANES), BF16)
    w2p = jnp.concatenate([w2_ref[...], zeros64], axis=0)
    a2p = jnp.concatenate([zeros64, a2_ref[...]], axis=0)
    wl = _dot(jnp.tanh(xwa).astype(BF16), w2p)
    al = _dot(xwa.astype(BF16), a2p)
    gate = _dot(_sigmoid(xg).astype(BF16), g2_ref[...])

    y = -(w0_ref[...] + wl)
    softplus = jnp.maximum(y, 0.0) + jnp.log(1.0 + jnp.exp(-jnp.abs(y)))
    lw = -jnp.exp(-softplus - 0.5)
    a = _sigmoid(a0_ref[...] + al)

    lane = lax.broadcasted_iota(jnp.int32, (LANES, LANES), 1)
    rowi = lax.broadcasted_iota(jnp.int32, (LANES, LANES), 0)
    same_head = (lane // RW_HEAD_DIM) == (rowi // RW_HEAD_DIM)
    head_ones = jnp.where(same_head, 1.0, 0.0).astype(F32)

    def head_sum(x):
        return jnp.dot(x, head_ones, preferred_element_type=F32, precision=HIGHEST)

    kk = k * kk_ref[...]
    kk = kk / jnp.maximum(jnp.sqrt(head_sum(kk * kk)), 1e-12)
    kmod = k * (1.0 + (a - 1.0) * ka_ref[...])

    sr_ref[...] = r
    sk_ref[...] = kmod
    sv_ref[...] = v
    sa_ref[...] = kk
    sb_ref[...] = kk * a
    sw_ref[...] = lw

    ri = lax.broadcasted_iota(jnp.int32, (c, c), 0)
    ci = lax.broadcasted_iota(jnp.int32, (c, c), 1)
    tri_incl = jnp.where(ci <= ri, 1.0, 0.0).astype(F32)
    lane_h0 = lax.broadcasted_iota(jnp.int32, (c, LANES), 1) < RW_HEAD_DIM
    strict = lane < rowi
    incl = lane <= rowi
    eye = lane == rowi

    def bd(x):
        return jnp.concatenate([jnp.where(lane_h0, x, 0.0), jnp.where(lane_h0, 0.0, x)], axis=0)

    def level_mask(m):
        return ((rowi // (2 * m)) == (lane // (2 * m))) & ((rowi % (2 * m)) >= m) & ((lane % (2 * m)) < m)

    def chunk_body(ic, carry):
        sl = pl.ds(pl.multiple_of(ic * c, c), c)
        rc, kc, vc = sr_ref[sl, :], sk_ref[sl, :], sv_ref[sl, :]
        ac, bc, lwc = sa_ref[sl, :], sb_ref[sl, :], sw_ref[sl, :]
        cum = jnp.dot(tri_incl, lwc, preferred_element_type=F32, precision=HIGHEST)
        cum_last = cum[c - 1:c, :]
        a_t = bd(ac * jnp.exp(cum - lwc)).astype(BF16)
        r_t = bd(rc * jnp.exp(cum))
        einv = jnp.exp(-cum)
        k_t = bd(kc * einv).astype(BF16)
        b_t = bd(bc * einv).astype(BF16)
        eend = jnp.exp(cum_last - cum)
        k_e = bd(kc * eend).astype(BF16)
        b_e = bd(bc * eend).astype(BF16)
        v_b = bd(vc).astype(BF16)
        w_end = jnp.exp(cum_last)

        sc = _dot_nt(jnp.concatenate([a_t, r_t.astype(BF16)], axis=0),
                     jnp.concatenate([k_t, b_t], axis=0))
        a_ak = jnp.where(strict, sc[:LANES, :LANES], 0.0)
        l_ab = jnp.where(strict, sc[:LANES, LANES:], 0.0)
        a_rk = jnp.where(incl, sc[LANES:, :LANES], 0.0)
        a_rb = jnp.where(incl, sc[LANES:, LANES:], 0.0)

        x = jnp.where(eye, 1.0, 0.0) - jnp.where(level_mask(1), l_ab, 0.0)
        for m in (2, 4, 8, 16, 32):
            cm = jnp.where(level_mask(m), l_ab, 0.0)
            xc = jnp.dot(x, cm, preferred_element_type=F32, precision=HIGHEST)
            x = x - jnp.dot(xc, x, preferred_element_type=F32, precision=HIGHEST)
        t_inv = x.astype(BF16)

        akv = _dot(a_ak.astype(BF16), v_b)
        ua = _dot(t_inv, jnp.concatenate([akv.astype(BF16), a_t], axis=1)).astype(BF16)
        rb_ua = _dot(a_rb.astype(BF16), ua)
        rkv = _dot(a_rk.astype(BF16), v_b)
        r_bar = r_t - rb_ua[:, LANES:]
        y0 = rkv - rb_ua[:, :LANES]
        bt_ua = _dot_tn(b_e, ua)
        ktv = _dot_tn(k_e, v_b)
        g_mat = jnp.where(eye, w_end, 0.0) - bt_ua[:, LANES:]
        h_mat = ktv - bt_ua[:, :LANES]

        p = p_ref[...]
        pb = p.astype(BF16)
        y_bd = _dot(r_bar.astype(BF16), pb) + y0
        sy_ref[sl, :] = y_bd[:c, :] + y_bd[c:, :]
        p_ref[...] = _dot(g_mat.astype(BF16), pb) + h_mat
        return carry

    lax.fori_loop(0, nc, chunk_body, 0)

    yv = sy_ref[...]
    inv_d = 1.0 / RW_HEAD_DIM
    mean = head_sum(yv) * inv_d
    yc = yv - mean
    var = head_sum(yc * yc) * inv_d
    yn = yc * lax.rsqrt(var + RW_GN_EPS) * gng_ref[...] + gnb_ref[...]
    bonus = head_sum(r * kmod * rk_ref[...]) * v
    o_ref[0] = ((yn + bonus) * gate).astype(o_ref.dtype)


def _wkv(z_rw, mu, w0, w2, a0, a2, g2, k_k, k_a, r_k, gn_g, gn_b, ts):
    b, s, _ = z_rw.shape
    npair = RW_WIDTH // LANES
    lo_blk = (3 * RW_WIDTH) // (2 * LANES)
    row = lambda a: a.reshape(1, -1)

    def col(off):
        return pl.BlockSpec((1, ts, LANES), lambda ib, ip, it: (ib, it, off + ip))

    def vec(off):
        return pl.BlockSpec((1, LANES), lambda ib, ip, it: (0, off + ip))

    in_specs = [
        col(0), col(npair), col(2 * npair),
        pl.BlockSpec((1, ts, 2 * LANES), lambda ib, ip, it: (ib, it, lo_blk)),
        vec(0), vec(npair), vec(2 * npair),
        pl.BlockSpec((1, 2 * LANES), lambda ib, ip, it: (0, lo_blk)),
        vec(0), vec(0), vec(0), vec(0), vec(0), vec(0), vec(0),
        pl.BlockSpec((RW_DECAY_LORA, LANES), lambda ib, ip, it: (0, ip)),
        pl.BlockSpec((RW_ICLR_LORA, LANES), lambda ib, ip, it: (0, ip)),
        pl.BlockSpec((RW_GATE_LORA, LANES), lambda ib, ip, it: (0, ip)),
    ]
    tile = pltpu.VMEM((ts, LANES), F32)
    scratch = [
        pltpu.VMEM((LANES, LANES), F32),
        pltpu.VMEM((1, LANES), F32), pltpu.VMEM((1, LANES), F32), pltpu.VMEM((1, LANES), F32),
        pltpu.VMEM((1, 2 * LANES), F32),
        tile, tile, tile, tile, tile, tile, tile,
    ]
    mu2 = row(mu)
    return pl.pallas_call(
        _wkv_kernel,
        grid=(b, npair, s // ts),
        in_specs=in_specs,
        out_specs=pl.BlockSpec((1, ts, LANES), lambda ib, ip, it: (ib, it, ip)),
        out_shape=jax.ShapeDtypeStruct((b, s, RW_WIDTH), BF16),
        scratch_shapes=scratch,
        compiler_params=_cparams(("parallel", "parallel", "arbitrary")),
        name="wkv",
    )(z_rw, z_rw, z_rw, z_rw, mu2, mu2, mu2, mu2,
      row(w0), row(a0), row(k_k), row(k_a), row(r_k), row(gn_g), row(gn_b),
      w2.astype(BF16), a2.astype(BF16), g2.astype(BF16))


def _attn_kernel(lq1_ref, lk1_ref, lq2_ref, lk2_ref, q_ref, k_ref, v_ref, subg_ref, o_ref,
                 m_ref, l_ref, acc_ref, *, lambda_init):
    tq = q_ref.shape[1]
    tk = k_ref.shape[1]
    qi = pl.program_id(2)
    ki = pl.program_id(3)

    @pl.when(ki == 0)
    def _():
        m_ref[...] = jnp.full_like(m_ref, -jnp.inf)
        l_ref[...] = jnp.zeros_like(l_ref)
        acc_ref[...] = jnp.zeros_like(acc_ref)

    @pl.when(ki <= qi)
    def _():
        q = q_ref[0]
        lane_h0 = lax.broadcasted_iota(jnp.int32, q.shape, 1) < DA_QK_DIM
        zero = jnp.zeros_like(q)
        qs = jnp.concatenate([jnp.where(lane_h0, q, zero), jnp.where(lane_h0, zero, q)], axis=0)
        s = _dot_nt(qs, k_ref[0]) * (DA_QK_DIM ** -0.5)
        qpos = qi * tq + lax.broadcasted_iota(jnp.int32, (2 * tq, tk), 0) % tq
        kpos = ki * tk + lax.broadcasted_iota(jnp.int32, (2 * tq, tk), 1)
        s = jnp.where((kpos // CHUNK) <= (qpos // CHUNK), s, -jnp.inf)
        m_prev = m_ref[...]
        m_new = jnp.maximum(m_prev, jnp.max(s, axis=-1, keepdims=True))
        alpha = jnp.exp(m_prev - m_new)
        p = jnp.exp(s - m_new)
        l_ref[...] = alpha * l_ref[...] + jnp.sum(p, axis=-1, keepdims=True)
        acc_ref[...] = alpha * acc_ref[...] + _dot(p.astype(BF16), v_ref[0])
        m_ref[...] = m_new

    @pl.when(ki == qi)
    def _():
        lam = (jnp.exp(jnp.sum(lq1_ref[...] * lk1_ref[...], axis=-1, keepdims=True))
               - jnp.exp(jnp.sum(lq2_ref[...] * lk2_ref[...], axis=-1, keepdims=True))
               + lambda_init)
        o = acc_ref[...] / l_ref[...]
        o = o[:tq, :] - lam * o[tq:, :]
        o = _rms(o, subg_ref[...], DA_SUBLN_EPS) * (1.0 - lambda_init)
        o_ref[0] = o.astype(o_ref.dtype)


def _attn(z_da, lq1, lk1, lq2, lk2, sub_g, lambda_init, tq):
    b, s, _ = z_da.shape
    tk = tq
    row = lambda a: a.reshape(1, -1)
    lam_spec = pl.BlockSpec((1, DA_QK_DIM), lambda ib, ih, iq, ik: (0, 0))
    kern = functools.partial(_attn_kernel, lambda_init=lambda_init)
    return pl.pallas_call(
        kern,
        grid=(b, DA_HEADS, s // tq, s // tk),
        in_specs=[
            lam_spec, lam_spec, lam_spec, lam_spec,
            pl.BlockSpec((1, tq, LANES), lambda ib, ih, iq, ik: (ib, iq, ih)),
            pl.BlockSpec((1, tk, LANES), lambda ib, ih, iq, ik: (ib, jnp.minimum(ik, iq), DA_HEADS + ih)),
            pl.BlockSpec((1, tk, LANES), lambda ib, ih, iq, ik: (ib, jnp.minimum(ik, iq), 2 * DA_HEADS + ih)),
            pl.BlockSpec((1, DA_V_DIM), lambda ib, ih, iq, ik: (0, 0)),
        ],
        out_specs=pl.BlockSpec((1, tq, LANES), lambda ib, ih, iq, ik: (ib, iq, ih)),
        out_shape=jax.ShapeDtypeStruct((b, s, DA_V_WIDTH), BF16),
        scratch_shapes=[
            pltpu.VMEM((2 * tq, 1), F32),
            pltpu.VMEM((2 * tq, 1), F32),
            pltpu.VMEM((2 * tq, DA_V_DIM), F32),
        ],
        compiler_params=_cparams(("parallel", "parallel", "parallel", "arbitrary")),
        name="attn",
    )(row(lq1), row(lk1), row(lq2), row(lk2), z_da, z_da, z_da, row(sub_g))


def _mix_kernel(x_ref, yrw_ref, yda_ref, grw_ref, gda_ref, wrw_ref, wda_ref, wo_ref, g_ref, o_ref):
    y_rw = _dot(yrw_ref[...], wrw_ref[...])
    y_da = _dot(yda_ref[...], wda_ref[...])
    merged = _sigmoid(grw_ref[...]) * y_rw + _sigmoid(gda_ref[...]) * y_da
    mo = _dot(merged.astype(BF16), wo_ref[...])
    o_ref[...] = x_ref[...] + _rms(mo, g_ref[...], RMS_EPS)


def _mix(x2d, y_rw, y_da, z_g, w_rw, w_da, w_o, g, tm):
    t, d = x2d.shape
    full = lambda shape: pl.BlockSpec(shape, lambda i: (0, 0))
    return pl.pallas_call(
        _mix_kernel,
        grid=(t // tm,),
        in_specs=[
            pl.BlockSpec((tm, d), lambda i: (i, 0)),
            pl.BlockSpec((tm, RW_WIDTH), lambda i: (i, 0)),
            pl.BlockSpec((tm, DA_V_WIDTH), lambda i: (i, 0)),
            pl.BlockSpec((tm, d), lambda i: (i, 0)),
            pl.BlockSpec((tm, d), lambda i: (i, 1)),
            full(w_rw.shape), full(w_da.shape), full(w_o.shape), full((1, d)),
        ],
        out_specs=pl.BlockSpec((tm, d), lambda i: (i, 0)),
        out_shape=jax.ShapeDtypeStruct((t, d), F32),
        compiler_params=_cparams(("parallel",)),
        name="mix",
    )(x2d, y_rw, y_da, z_g, z_g, w_rw, w_da, w_o, g)


def _mlp_kernel(x_ref, gpre_ref, wup_ref, wdn_ref, gpost_ref, o_ref, h_ref, acc_ref):
    j = pl.program_id(1)

    @pl.when(j == 0)
    def _():
        h_ref[...] = _rms(x_ref[...], gpre_ref[...], RMS_EPS).astype(BF16)
        acc_ref[...] = jnp.zeros_like(acc_ref)

    up = _dot(h_ref[...], wup_ref[...])
    act = jnp.square(jnp.maximum(up, 0.0))
    acc_ref[...] += _dot(act.astype(BF16), wdn_ref[...])

    @pl.when(j == pl.num_programs(1) - 1)
    def _():
        o_ref[...] = x_ref[...] + _rms(acc_ref[...], gpost_ref[...], RMS_EPS)


def _mlp(x2d, g_pre, w_up, w_dn, g_post, tm, tf):
    t, d = x2d.shape
    ff = w_up.shape[1]
    return pl.pallas_call(
        _mlp_kernel,
        grid=(t // tm, ff // tf),
        in_specs=[
            pl.BlockSpec((tm, d), lambda i, j: (i, 0)),
            pl.BlockSpec((1, d), lambda i, j: (0, 0)),
            pl.BlockSpec((d, tf), lambda i, j: (0, j)),
            pl.BlockSpec((tf, d), lambda i, j: (j, 0)),
            pl.BlockSpec((1, d), lambda i, j: (0, 0)),
        ],
        out_specs=pl.BlockSpec((tm, d), lambda i, j: (i, 0)),
        out_shape=jax.ShapeDtypeStruct((t, d), F32),
        scratch_shapes=[pltpu.VMEM((tm, d), BF16), pltpu.VMEM((tm, d), F32)],
        compiler_params=_cparams(("parallel", "arbitrary")),
        name="mlp",
    )(x2d, g_pre, w_up, w_dn, g_post)


def _tile(n, want):
    t = min(n, want)
    while n % t:
        t //= 2
    return t


def kernel(x, pre_mix_g, w_in, rw_shift_mu, rw_w0, rw_w2, rw_a0, rw_a2, rw_g2, rw_k_k, rw_k_a, rw_r_k, rw_gn_g, rw_gn_b, w_rw_out, da_lam_q1, da_lam_k1, da_lam_q2, da_lam_k2, da_sub_g, w_da_out, w_o, post_mix_g, pre_mlp_g, w_up, w_down, post_mlp_g):
    b, s, d = x.shape
    depth = w_in.shape[0]
    t = b * s
    tm = _tile(t, 512)
    for l in range(depth):
        lambda_init = 0.8 - 0.6 * math.exp(-0.3 * l)
        x2d = x.reshape(t, d)
        g_pre = pre_mix_g[l].reshape(1, d)
        w_in_l = w_in[l].astype(BF16)
        z_rw = _inproj(x2d, g_pre, w_in_l[:, :RW_IN], F32, tm, RW_IN // 2)
        z_da = _inproj(x2d, g_pre, w_in_l[:, RW_IN:RW_IN + DA_IN], BF16, tm, DA_IN // 2)
        z_g = _inproj(x2d, g_pre, w_in_l[:, RW_IN + DA_IN:], F32, tm, d)

        y_rw = _wkv(z_rw.reshape(b, s, RW_IN), rw_shift_mu[l], rw_w0[l], rw_w2[l], rw_a0[l],
                    rw_a2[l], rw_g2[l], rw_k_k[l], rw_k_a[l], rw_r_k[l], rw_gn_g[l], rw_gn_b[l],
                    _tile(s, 512))
        y_da = _attn(z_da.reshape(b, s, DA_IN), da_lam_q1[l], da_lam_k1[l], da_lam_q2[l],
                     da_lam_k2[l], da_sub_g[l], lambda_init, _tile(s, 512))

        x1 = _mix(x2d, y_rw.reshape(t, RW_WIDTH), y_da.reshape(t, DA_V_WIDTH), z_g,
                  w_rw_out[l].astype(BF16), w_da_out[l].astype(BF16), w_o[l].astype(BF16),
                  post_mix_g[l].reshape(1, d), tm)
        x2 = _mlp(x1, pre_mlp_g[l].reshape(1, d), w_up[l].astype(BF16), w_down[l].astype(BF16),
                  post_mlp_g[l].reshape(1, d), _tile(t, 1024), 1024)
        x = x2.reshape(b, s, d)
    return x
```

```python
import functools
import math

import jax
import jax.numpy as jnp
from jax import lax
from jax.experimental import pallas as pl
from jax.experimental.pallas import tpu as pltpu

F32 = jnp.float32
BF16 = jnp.bfloat16

RMS_EPS = 1e-6
CHUNK = 64
RW_HEADS = 8
RW_HEAD_DIM = 64
RW_WIDTH = RW_HEADS * RW_HEAD_DIM
RW_DECAY_LORA = 64
RW_ICLR_LORA = 64
RW_GATE_LORA = 128
RW_GN_EPS = 1e-5 * RW_HEAD_DIM
RW_IN = 3 * RW_WIDTH + RW_DECAY_LORA + RW_ICLR_LORA + RW_GATE_LORA
DA_HEADS = 4
DA_QK_DIM = 64
DA_V_DIM = 2 * DA_QK_DIM
DA_QK_WIDTH = DA_HEADS * 2 * DA_QK_DIM
DA_V_WIDTH = DA_HEADS * DA_V_DIM
DA_IN = 2 * DA_QK_WIDTH + DA_V_WIDTH
DA_SUBLN_EPS = 1e-5

LANES = 128
WKV_CHUNK = 64
VMEM_LIMIT = 56 * 1024 * 1024

HIGHEST = lax.Precision.HIGHEST


def _cparams(semantics):
    return pltpu.CompilerParams(dimension_semantics=semantics, vmem_limit_bytes=VMEM_LIMIT)


def _rms(x, g, eps):
    return x * lax.rsqrt(jnp.mean(x * x, axis=-1, keepdims=True) + eps) * g


def _dot(a, b):
    return jnp.dot(a, b, preferred_element_type=F32)


def _dot_nt(a, b):
    return lax.dot_general(a, b, (((1,), (1,)), ((), ())), preferred_element_type=F32)


def _dot_tn(a, b):
    return lax.dot_general(a, b, (((0,), (0,)), ((), ())), preferred_element_type=F32)


def _inproj_kernel(x_ref, g_ref, w_ref, o_ref):
    h = _rms(x_ref[...], g_ref[...], RMS_EPS)
    o_ref[...] = _dot(h.astype(BF16), w_ref[...]).astype(o_ref.dtype)


def _inproj(x2d, g, w_bf16, out_dtype, tm, tn):
    t, d = x2d.shape
    n = w_bf16.shape[1]
    return pl.pallas_call(
        _inproj_kernel,
        grid=(n // tn, t // tm),
        in_specs=[
            pl.BlockSpec((tm, d), lambda j, i: (i, 0)),
            pl.BlockSpec((1, d), lambda j, i: (0, 0)),
            pl.BlockSpec((d, tn), lambda j, i: (0, j)),
        ],
        out_specs=pl.BlockSpec((tm, tn), lambda j, i: (i, j)),
        out_shape=jax.ShapeDtypeStruct((t, n), out_dtype),
        compiler_params=_cparams(("parallel", "parallel")),
        name="inproj",
    )(x2d, g, w_bf16)


def _sigmoid(x):
    return 1.0 / (1.0 + jnp.exp(-x))


def _wkv_kernel(r_ref, k_ref, v_ref, lo_ref, mur_ref, muk_ref, muv_ref, mulo_ref,
                w0_ref, a0_ref, kk_ref, ka_ref, rk_ref, gng_ref, gnb_ref,
                w2_ref, a2_ref, g2_ref, o_ref,
                p_ref, cr_ref, ck_ref, cv_ref, clo_ref,
                sr_ref, sk_ref, sv_ref, sa_ref, sb_ref, sw_ref, sy_ref):
    ts = r_ref.shape[1]
    nc = ts // WKV_CHUNK
    c = WKV_CHUNK

    @pl.when(pl.program_id(2) == 0)
    def _():
        p_ref[...] = jnp.zeros_like(p_ref)
        cr_ref[...] = jnp.zeros_like(cr_ref)
        ck_ref[...] = jnp.zeros_like(ck_ref)
        cv_ref[...] = jnp.zeros_like(cv_ref)
        clo_ref[...] = jnp.zeros_like(clo_ref)

    def shifted(z, carry_ref, mu):
        prev = pltpu.roll(z, 1, axis=0)
        row = lax.broadcasted_iota(jnp.int32, z.shape, 0)
        prev = jnp.where(row == 0, carry_ref[...], prev)
        carry_ref[...] = z[ts - 1:ts, :]
        return z + (prev - z) * mu

    r = shifted(r_ref[0], cr_ref, mur_ref[...])
    k = shifted(k_ref[0], ck_ref, muk_ref[...])
    v = shifted(v_ref[0], cv_ref, muv_ref[...])
    lo = shifted(lo_ref[0], clo_ref, mulo_ref[...])
    xwa = lo[:, :LANES]
    xg = lo[:, LANES:]

    zeros64 = jnp.zeros((RW_DECAY_LORA, LANES), BF16)
    w2p = jnp.concatenate([w2_ref[...], zeros64], axis=0)
    a2p = jnp.concatenate([zeros64, a2_ref[...]], axis=0)
    wl = _dot(jnp.tanh(xwa).astype(BF16), w2p)
    al = _dot(xwa.astype(BF16), a2p)
    gate = _dot(_sigmoid(xg).astype(BF16), g2_ref[...])

    y = -(w0_ref[...] + wl)
    softplus = jnp.maximum(y, 0.0) + jnp.log(1.0 + jnp.exp(-jnp.abs(y)))
    lw = -jnp.exp(-softplus - 0.5)
    a = _sigmoid(a0_ref[...] + al)

    lane = lax.broadcasted_iota(jnp.int32, (LANES, LANES), 1)
    rowi = lax.broadcasted_iota(jnp.int32, (LANES, LANES), 0)
    same_head = (lane // RW_HEAD_DIM) == (rowi // RW_HEAD_DIM)
    head_ones = jnp.where(same_head, 1.0, 0.0).astype(F32)

    def head_sum(x):
        return jnp.dot(x, head_ones, preferred_element_type=F32, precision=HIGHEST)

    kk = k * kk_ref[...]
    kk = kk / jnp.maximum(jnp.sqrt(head_sum(kk * kk)), 1e-12)
    kmod = k * (1.0 + (a - 1.0) * ka_ref[...])

    sr_ref[...] = r
    sk_ref[...] = kmod
    sv_ref[...] = v
    sa_ref[...] = kk
    sb_ref[...] = kk * a
    sw_ref[...] = lw

    ri = lax.broadcasted_iota(jnp.int32, (c, c), 0)
    ci = lax.broadcasted_iota(jnp.int32, (c, c), 1)
    tri_incl = jnp.where(ci <= ri, 1.0, 0.0).astype(F32)
    lane_h0 = lax.broadcasted_iota(jnp.int32, (c, LANES), 1) < RW_HEAD_DIM
    strict = lane < rowi
    incl = lane <= rowi
    eye = lane == rowi

    def bd(x):
        return jnp.concatenate([jnp.where(lane_h0, x, 0.0), jnp.where(lane_h0, 0.0, x)], axis=0)

    def level_mask(m):
        return ((rowi // (2 * m)) == (lane // (2 * m))) & ((rowi % (2 * m)) >= m) & ((lane % (2 * m)) < m)

    def chunk_body(ic, carry):
        sl = pl.ds(pl.multiple_of(ic * c, c), c)
        rc, kc, vc = sr_ref[sl, :], sk_ref[sl, :], sv_ref[sl, :]
        ac, bc, lwc = sa_ref[sl, :], sb_ref[sl, :], sw_ref[sl, :]
        cum = jnp.dot(tri_incl, lwc, preferred_element_type=F32, precision=HIGHEST)
        cum_last = cum[c - 1:c, :]
        a_t = bd(ac * jnp.exp(cum - lwc)).astype(BF16)
        r_t = bd(rc * jnp.exp(cum))
        einv = jnp.exp(-cum)
        k_t = bd(kc * einv).astype(BF16)
        b_t = bd(bc * einv).astype(BF16)
        eend = jnp.exp(cum_last - cum)
        k_e = bd(kc * eend).astype(BF16)
        b_e = bd(bc * eend).astype(BF16)
        v_b = bd(vc).astype(BF16)
        w_end = jnp.exp(cum_last)

        sc = _dot_nt(jnp.concatenate([a_t, r_t.astype(BF16)], axis=0),
                     jnp.concatenate([k_t, b_t], axis=0))
        a_ak = jnp.where(strict, sc[:LANES, :LANES], 0.0)
        l_ab = jnp.where(strict, sc[:LANES, LANES:], 0.0)
        a_rk = jnp.where(incl, sc[LANES:, :LANES], 0.0)
        a_rb = jnp.where(incl, sc[LANES:, LANES:], 0.0)

        x = jnp.where(eye, 1.0, 0.0) - jnp.where(level_mask(1), l_ab, 0.0)
        for m in (2, 4, 8, 16, 32):
            cm = jnp.where(level_mask(m), l_ab, 0.0)
            xc = jnp.dot(x, cm, preferred_element_type=F32, precision=HIGHEST)
            x = x - jnp.dot(xc, x, preferred_element_type=F32, precision=HIGHEST)
        t_inv = x.astype(BF16)

        akv = _dot(a_ak.astype(BF16), v_b)
        ua = _dot(t_inv, jnp.concatenate([akv.astype(BF16), a_t], axis=1)).astype(BF16)
        rb_ua = _dot(a_rb.astype(BF16), ua)
        rkv = _dot(a_rk.astype(BF16), v_b)
        r_bar = r_t - rb_ua[:, LANES:]
        y0 = rkv - rb_ua[:, :LANES]
        bt_ua = _dot_tn(b_e, ua)
        ktv = _dot_tn(k_e, v_b)
        g_mat = jnp.where(eye, w_end, 0.0) - bt_ua[:, LANES:]
        h_mat = ktv - bt_ua[:, :LANES]

        p = p_ref[...]
        pb = p.astype(BF16)
        y_bd = _dot(r_bar.astype(BF16), pb) + y0
        sy_ref[sl, :] = y_bd[:c, :] + y_bd[c:, :]
        p_ref[...] = _dot(g_mat.astype(BF16), pb) + h_mat
        return carry

    lax.fori_loop(0, nc, chunk_body, 0)

    yv = sy_ref[...]
    inv_d = 1.0 / RW_HEAD_DIM
    mean = head_sum(yv) * inv_d
    yc = yv - mean
    var = head_sum(yc * yc) * inv_d
    yn = yc * lax.rsqrt(var + RW_GN_EPS) * gng_ref[...] + gnb_ref[...]
    bonus = head_sum(r * kmod * rk_ref[...]) * v
    o_ref[0] = ((yn + bonus) * gate).astype(o_ref.dtype)


def _wkv(z_rw, mu, w0, w2, a0, a2, g2, k_k, k_a, r_k, gn_g, gn_b, ts):
    b, s, _ = z_rw.shape
    npair = RW_WIDTH // LANES
    lo_blk = (3 * RW_WIDTH) // (2 * LANES)
    row = lambda a: a.reshape(1, -1)

    def col(off):
        return pl.BlockSpec((1, ts, LANES), lambda ib, ip, it: (ib, it, off + ip))

    def vec(off):
        return pl.BlockSpec((1, LANES), lambda ib, ip, it: (0, off + ip))

    in_specs = [
        col(0), col(npair), col(2 * npair),
        pl.BlockSpec((1, ts, 2 * LANES), lambda ib, ip, it: (ib, it, lo_blk)),
        vec(0), vec(npair), vec(2 * npair),
        pl.BlockSpec((1, 2 * LANES), lambda ib, ip, it: (0, lo_blk)),
        vec(0), vec(0), vec(0), vec(0), vec(0), vec(0), vec(0),
        pl.BlockSpec((RW_DECAY_LORA, LANES), lambda ib, ip, it: (0, ip)),
        pl.BlockSpec((RW_ICLR_LORA, LANES), lambda ib, ip, it: (0, ip)),
        pl.BlockSpec((RW_GATE_LORA, LANES), lambda ib, ip, it: (0, ip)),
    ]
    tile = pltpu.VMEM((ts, LANES), F32)
    scratch = [
        pltpu.VMEM((LANES, LANES), F32),
        pltpu.VMEM((1, LANES), F32), pltpu.VMEM((1, LANES), F32), pltpu.VMEM((1, LANES), F32),
        pltpu.VMEM((1, 2 * LANES), F32),
        tile, tile, tile, tile, tile, tile, tile,
    ]
    mu2 = row(mu)
    return pl.pallas_call(
        _wkv_kernel,
        grid=(b, npair, s // ts),
        in_specs=in_specs,
        out_specs=pl.BlockSpec((1, ts, LANES), lambda ib, ip, it: (ib, it, ip)),
        out_shape=jax.ShapeDtypeStruct((b, s, RW_WIDTH), BF16),
        scratch_shapes=scratch,
        compiler_params=_cparams(("parallel", "parallel", "arbitrary")),
        name="wkv",
    )(z_rw, z_rw, z_rw, z_rw, mu2, mu2, mu2, mu2,
      row(w0), row(a0), row(k_k), row(k_a), row(r_k), row(gn_g), row(gn_b),
      w2.astype(BF16), a2.astype(BF16), g2.astype(BF16))


ATTN_ROWS = 16


def _attn_kernel(lq1_ref, lk1_ref, lq2_ref, lk2_ref, q_ref, k_ref, v_ref, subg_ref, o_ref,
                 qs_ref, s_ref, p_ref, m_ref, l_ref, alpha_ref, acc_ref, *, lambda_init, tk):
    tq = q_ref.shape[1]
    rows = 2 * tq
    qi = pl.program_id(2)
    ngroups = rows // ATTN_ROWS

    q = q_ref[0]
    lane_h0 = lax.broadcasted_iota(jnp.int32, q.shape, 1) < DA_QK_DIM
    zero = jnp.zeros_like(q)
    qsc = q * (DA_QK_DIM ** -0.5)
    qs_ref[...] = jnp.concatenate([jnp.where(lane_h0, qsc, zero), jnp.where(lane_h0, zero, qsc)], axis=0)
    m_ref[...] = jnp.full_like(m_ref, -jnp.inf)
    l_ref[...] = jnp.zeros_like(l_ref)
    acc_ref[...] = jnp.zeros_like(acc_ref)

    key_chunk = lax.broadcasted_iota(jnp.int32, (ATTN_ROWS, LANES), 1) // CHUNK

    def kv_block(j, masked):
        koff = pl.multiple_of(j * tk, tk)
        s_ref[...] = _dot_nt(qs_ref[...], k_ref[0, pl.ds(koff, tk), :])

        def load_scores(sl, r0, t):
            s = s_ref[sl, t * LANES:(t + 1) * LANES]
            if masked:
                q_chunk = (qi * tq + r0 % tq) // CHUNK - j * (tk // CHUNK) - t * (LANES // CHUNK)
                s = jnp.where(key_chunk <= q_chunk, s, -jnp.inf)
            return s

        def row_max(g, carry):
            r0 = pl.multiple_of(g * ATTN_ROWS, ATTN_ROWS)
            sl = pl.ds(r0, ATTN_ROWS)
            m_prev = m_ref[sl, :]
            tile_max = load_scores(sl, r0, 0)
            for t in range(1, tk // LANES):
                tile_max = jnp.maximum(tile_max, load_scores(sl, r0, t))
            m_new = jnp.maximum(m_prev, jnp.max(tile_max, axis=-1, keepdims=True))
            alpha_ref[sl, :] = jnp.exp(m_prev - m_new)
            m_ref[sl, :] = m_new
            return carry

        def row_exp(g, carry):
            r0 = pl.multiple_of(g * ATTN_ROWS, ATTN_ROWS)
            sl = pl.ds(r0, ATTN_ROWS)
            m_row = m_ref[sl, :]
            part = alpha_ref[sl, :] * l_ref[sl, :]
            for t in range(tk // LANES):
                p = jnp.exp(load_scores(sl, r0, t) - m_row)
                part = part + p
                p_ref[sl, t * LANES:(t + 1) * LANES] = p.astype(BF16)
            l_ref[sl, :] = part
            return carry

        lax.fori_loop(0, ngroups, row_max, 0, unroll=16)
        lax.fori_loop(0, ngroups, row_exp, 0, unroll=4)
        acc_ref[...] = alpha_ref[...] * acc_ref[...] + _dot(p_ref[...], v_ref[0, pl.ds(koff, tk), :])

    def run_blocks(lo, hi, masked):
        def body(j, carry):
            kv_block(j, masked)
            return carry
        lax.fori_loop(lo, hi, body, 0)

    nfull = (qi * tq + CHUNK) // tk
    nlast = (qi * tq + tq - 1) // tk + 1
    run_blocks(0, nfull, False)
    run_blocks(nfull, nlast, True)

    lam = (jnp.exp(jnp.sum(lq1_ref[...] * lk1_ref[...], axis=-1, keepdims=True))
           - jnp.exp(jnp.sum(lq2_ref[...] * lk2_ref[...], axis=-1, keepdims=True))
           + lambda_init)
    o = acc_ref[...] / jnp.sum(l_ref[...], axis=-1, keepdims=True)
    o = o[:tq, :] - lam * o[tq:, :]
    o = _rms(o, subg_ref[...], DA_SUBLN_EPS) * (1.0 - lambda_init)
    o_ref[0] = o.astype(o_ref.dtype)


def _attn(z_da, lq1, lk1, lq2, lk2, sub_g, lambda_init, tq, tk):
    b, s, _ = z_da.shape
    row = lambda a: a.reshape(1, -1)
    lam_spec = pl.BlockSpec((1, DA_QK_DIM), lambda ib, ih, iq: (0, 0))
    kern = functools.partial(_attn_kernel, lambda_init=lambda_init, tk=tk)
    rep = pltpu.VMEM((2 * tq, LANES), F32)
    return pl.pallas_call(
        kern,
        grid=(b, DA_HEADS, s // tq),
        in_specs=[
            lam_spec, lam_spec, lam_spec, lam_spec,
            pl.BlockSpec((1, tq, LANES), lambda ib, ih, iq: (ib, iq, ih)),
            pl.BlockSpec((1, s, LANES), lambda ib, ih, iq: (ib, 0, DA_HEADS + ih)),
            pl.BlockSpec((1, s, LANES), lambda ib, ih, iq: (ib, 0, 2 * DA_HEADS + ih)),
            pl.BlockSpec((1, DA_V_DIM), lambda ib, ih, iq: (0, 0)),
        ],
        out_specs=pl.BlockSpec((1, tq, LANES), lambda ib, ih, iq: (ib, iq, ih)),
        out_shape=jax.ShapeDtypeStruct((b, s, DA_V_WIDTH), BF16),
        scratch_shapes=[
            pltpu.VMEM((2 * tq, LANES), BF16),
            pltpu.VMEM((2 * tq, tk), F32),
            pltpu.VMEM((2 * tq, tk), BF16),
            rep, rep, rep,
            pltpu.VMEM((2 * tq, DA_V_DIM), F32),
        ],
        compiler_params=_cparams(("parallel", "parallel", "arbitrary")),
        name="attn",
    )(row(lq1), row(lk1), row(lq2), row(lk2), z_da, z_da, z_da, row(sub_g))


def _mix_kernel(x_ref, yrw_ref, yda_ref, grw_ref, gda_ref, wrw_ref, wda_ref, wo_ref, g_ref, o_ref):
    y_rw = _dot(yrw_ref[...], wrw_ref[...])
    y_da = _dot(yda_ref[...], wda_ref[...])
    merged = _sigmoid(grw_ref[...]) * y_rw + _sigmoid(gda_ref[...]) * y_da
    mo = _dot(merged.astype(BF16), wo_ref[...])
    o_ref[...] = x_ref[...] + _rms(mo, g_ref[...], RMS_EPS)


def _mix(x2d, y_rw, y_da, z_g, w_rw, w_da, w_o, g, tm):
    t, d = x2d.shape
    full = lambda shape: pl.BlockSpec(shape, lambda i: (0, 0))
    return pl.pallas_call(
        _mix_kernel,
        grid=(t // tm,),
        in_specs=[
            pl.BlockSpec((tm, d), lambda i: (i, 0)),
            pl.BlockSpec((tm, RW_WIDTH), lambda i: (i, 0)),
            pl.BlockSpec((tm, DA_V_WIDTH), lambda i: (i, 0)),
            pl.BlockSpec((tm, d), lambda i: (i, 0)),
            pl.BlockSpec((tm, d), lambda i: (i, 1)),
            full(w_rw.shape), full(w_da.shape), full(w_o.shape), full((1, d)),
        ],
        out_specs=pl.BlockSpec((tm, d), lambda i: (i, 0)),
        out_shape=jax.ShapeDtypeStruct((t, d), F32),
        compiler_params=_cparams(("parallel",)),
        name="mix",
    )(x2d, y_rw, y_da, z_g, z_g, w_rw, w_da, w_o, g)


def _mlp_kernel(x_ref, gpre_ref, wup_ref, wdn_ref, gpost_ref, o_ref, h_ref, acc_ref):
    j = pl.program_id(1)

    @pl.when(j == 0)
    def _():
        h_ref[...] = _rms(x_ref[...], gpre_ref[...], RMS_EPS).astype(BF16)
        acc_ref[...] = jnp.zeros_like(acc_ref)

    up = _dot(h_ref[...], wup_ref[...])
    act = jnp.square(jnp.maximum(up, 0.0))
    acc_ref[...] += _dot(act.astype(BF16), wdn_ref[...])

    @pl.when(j == pl.num_programs(1) - 1)
    def _():
        o_ref[...] = x_ref[...] + _rms(acc_ref[...], gpost_ref[...], RMS_EPS)


def _mlp(x2d, g_pre, w_up, w_dn, g_post, tm, tf):
    t, d = x2d.shape
    ff = w_up.shape[1]
    return pl.pallas_call(
        _mlp_kernel,
        grid=(t // tm, ff // tf),
        in_specs=[
            pl.BlockSpec((tm, d), lambda i, j: (i, 0)),
            pl.BlockSpec((1, d), lambda i, j: (0, 0)),
            pl.BlockSpec((d, tf), lambda i, j: (0, j)),
            pl.BlockSpec((tf, d), lambda i, j: (j, 0)),
            pl.BlockSpec((1, d), lambda i, j: (0, 0)),
        ],
        out_specs=pl.BlockSpec((tm, d), lambda i, j: (i, 0)),
        out_shape=jax.ShapeDtypeStruct((t, d), F32),
        scratch_shapes=[pltpu.VMEM((tm, d), BF16), pltpu.VMEM((tm, d), F32)],
        compiler_params=_cparams(("parallel", "arbitrary")),
        name="mlp",
    )(x2d, g_pre, w_up, w_dn, g_post)


def _tile(n, want):
    t = min(n, want)
    while n % t:
        t //= 2
    return t


def kernel(x, pre_mix_g, w_in, rw_shift_mu, rw_w0, rw_w2, rw_a0, rw_a2, rw_g2, rw_k_k, rw_k_a, rw_r_k, rw_gn_g, rw_gn_b, w_rw_out, da_lam_q1, da_lam_k1, da_lam_q2, da_lam_k2, da_sub_g, w_da_out, w_o, post_mix_g, pre_mlp_g, w_up, w_down, post_mlp_g):
    b, s, d = x.shape
    depth = w_in.shape[0]
    t = b * s
    tm = _tile(t, 512)
    for l in range(depth):
        lambda_init = 0.8 - 0.6 * math.exp(-0.3 * l)
        x2d = x.reshape(t, d)
        g_pre = pre_mix_g[l].reshape(1, d)
        w_in_l = w_in[l].astype(BF16)
        z_rw = _inproj(x2d, g_pre, w_in_l[:, :RW_IN], F32, tm, RW_IN // 2)
        z_da = _inproj(x2d, g_pre, w_in_l[:, RW_IN:RW_IN + DA_IN], BF16, tm, DA_IN // 2)
        z_g = _inproj(x2d, g_pre, w_in_l[:, RW_IN + DA_IN:], F32, tm, d)

        y_rw = _wkv(z_rw.reshape(b, s, RW_IN), rw_shift_mu[l], rw_w0[l], rw_w2[l], rw_a0[l],
                    rw_a2[l], rw_g2[l], rw_k_k[l], rw_k_a[l], rw_r_k[l], rw_gn_g[l], rw_gn_b[l],
                    _tile(s, 512))
        y_da = _attn(z_da.reshape(b, s, DA_IN), da_lam_q1[l], da_lam_k1[l], da_lam_q2[l],
                     da_lam_k2[l], da_sub_g[l], lambda_init, _tile(s, 512), _tile(s, 1024))

        x1 = _mix(x2d, y_rw.reshape(t, RW_WIDTH), y_da.reshape(t, DA_V_WIDTH), z_g,
                  w_rw_out[l].astype(BF16), w_da_out[l].astype(BF16), w_o[l].astype(BF16),
                  post_mix_g[l].reshape(1, d), tm)
        x2 = _mlp(x1, pre_mlp_g[l].reshape(1, d), w_up[l].astype(BF16), w_down[l].astype(BF16),
                  post_mlp_g[l].reshape(1, d), _tile(t, 1024), 1024)
        x = x2.reshape(b, s, d)
    return x
```

```python
import functools
import math

import jax
import jax.numpy as jnp
from jax import lax
from jax.experimental import pallas as pl
from jax.experimental.pallas import tpu as pltpu

F32 = jnp.float32
BF16 = jnp.bfloat16

RMS_EPS = 1e-6
CHUNK = 64
RW_HEADS = 8
RW_HEAD_DIM = 64
RW_WIDTH = RW_HEADS * RW_HEAD_DIM
RW_DECAY_LORA = 64
RW_ICLR_LORA = 64
RW_GATE_LORA = 128
RW_GN_EPS = 1e-5 * RW_HEAD_DIM
RW_IN = 3 * RW_WIDTH + RW_DECAY_LORA + RW_ICLR_LORA + RW_GATE_LORA
DA_HEADS = 4
DA_QK_DIM = 64
DA_V_DIM = 2 * DA_QK_DIM
DA_QK_WIDTH = DA_HEADS * 2 * DA_QK_DIM
DA_V_WIDTH = DA_HEADS * DA_V_DIM
DA_IN = 2 * DA_QK_WIDTH + DA_V_WIDTH
DA_SUBLN_EPS = 1e-5

LANES = 128
WKV_CHUNK = 64
VMEM_LIMIT = 56 * 1024 * 1024

HIGHEST = lax.Precision.HIGHEST


def _cparams(semantics):
    return pltpu.CompilerParams(dimension_semantics=semantics, vmem_limit_bytes=VMEM_LIMIT)


def _rms(x, g, eps):
    return x * lax.rsqrt(jnp.mean(x * x, axis=-1, keepdims=True) + eps) * g


def _dot(a, b):
    return jnp.dot(a, b, preferred_element_type=F32)


def _dot_nt(a, b):
    return lax.dot_general(a, b, (((1,), (1,)), ((), ())), preferred_element_type=F32)


def _dot_tn(a, b):
    return lax.dot_general(a, b, (((0,), (0,)), ((), ())), preferred_element_type=F32)


def _inproj_kernel(x_ref, g_ref, w_ref, o_ref):
    h = _rms(x_ref[...], g_ref[...], RMS_EPS)
    o_ref[...] = _dot(h.astype(BF16), w_ref[...]).astype(o_ref.dtype)


def _inproj(x2d, g, w_bf16, out_dtype, tm, tn):
    t, d = x2d.shape
    n = w_bf16.shape[1]
    return pl.pallas_call(
        _inproj_kernel,
        grid=(n // tn, t // tm),
        in_specs=[
            pl.BlockSpec((tm, d), lambda j, i: (i, 0)),
            pl.BlockSpec((1, d), lambda j, i: (0, 0)),
            pl.BlockSpec((d, tn), lambda j, i: (0, j)),
        ],
        out_specs=pl.BlockSpec((tm, tn), lambda j, i: (i, j)),
        out_shape=jax.ShapeDtypeStruct((t, n), out_dtype),
        compiler_params=_cparams(("parallel", "parallel")),
        name="inproj",
    )(x2d, g, w_bf16)


def _sigmoid(x):
    return 1.0 / (1.0 + jnp.exp(-x))


def _inv_dot(a, b):
    return _dot(a.astype(BF16), b.astype(BF16))


def _wkv_kernel(r_ref, k_ref, v_ref, lo_ref, mur_ref, muk_ref, muv_ref, mulo_ref,
                w0_ref, a0_ref, kk_ref, ka_ref, rk_ref, gng_ref, gnb_ref,
                w2_ref, a2_ref, g2_ref, o_ref,
                p_ref, cr_ref, ck_ref, cv_ref, clo_ref):
    ts = r_ref.shape[1]
    nc = ts // WKV_CHUNK
    c = WKV_CHUNK

    @pl.when(pl.program_id(2) == 0)
    def _():
        p_ref[...] = jnp.zeros_like(p_ref)
        cr_ref[...] = jnp.zeros_like(cr_ref)
        ck_ref[...] = jnp.zeros_like(ck_ref)
        cv_ref[...] = jnp.zeros_like(cv_ref)
        clo_ref[...] = jnp.zeros_like(clo_ref)

    def shifted(z, carry_ref, mu):
        prev = pltpu.roll(z, 1, axis=0)
        row = lax.broadcasted_iota(jnp.int32, z.shape, 0)
        prev = jnp.where(row == 0, carry_ref[...], prev)
        carry_ref[...] = z[ts - 1:ts, :]
        return z + (prev - z) * mu

    r = shifted(r_ref[0], cr_ref, mur_ref[...])
    k = shifted(k_ref[0], ck_ref, muk_ref[...])
    v = shifted(v_ref[0], cv_ref, muv_ref[...])
    lo = shifted(lo_ref[0], clo_ref, mulo_ref[...])
    xwa = lo[:, :LANES]
    xg = lo[:, LANES:]

    zeros64 = jnp.zeros((RW_DECAY_LORA, LANES), BF16)
    w2p = jnp.concatenate([w2_ref[...], zeros64], axis=0)
    a2p = jnp.concatenate([zeros64, a2_ref[...]], axis=0)
    wl = _dot(jnp.tanh(xwa).astype(BF16), w2p)
    al = _dot(xwa.astype(BF16), a2p)
    gate = _dot(_sigmoid(xg).astype(BF16), g2_ref[...])

    y = -(w0_ref[...] + wl)
    softplus = jnp.maximum(y, 0.0) + jnp.log(1.0 + jnp.exp(-jnp.abs(y)))
    lw = -jnp.exp(-softplus - 0.5)
    a = _sigmoid(a0_ref[...] + al)

    lane = lax.broadcasted_iota(jnp.int32, (LANES, LANES), 1)
    rowi = lax.broadcasted_iota(jnp.int32, (LANES, LANES), 0)
    same_head = (lane // RW_HEAD_DIM) == (rowi // RW_HEAD_DIM)
    head_ones = jnp.where(same_head, 1.0, 0.0).astype(F32)

    def head_sum(x):
        return jnp.dot(x, head_ones, preferred_element_type=F32, precision=HIGHEST)

    kk = k * kk_ref[...]
    kk = kk / jnp.maximum(jnp.sqrt(head_sum(kk * kk)), 1e-12)
    kmod = k * (1.0 + (a - 1.0) * ka_ref[...])

    bb = kk * a

    ri = lax.broadcasted_iota(jnp.int32, (c, c), 0)
    ci = lax.broadcasted_iota(jnp.int32, (c, c), 1)
    tri_incl = jnp.where(ci <= ri, 1.0, 0.0).astype(F32)
    lane_h0 = lax.broadcasted_iota(jnp.int32, (c, LANES), 1) < RW_HEAD_DIM
    strict = lane < rowi
    incl = lane <= rowi
    eye = lane == rowi
    eye_f = jnp.where(eye, 1.0, 0.0)

    def bd(x):
        return jnp.concatenate([jnp.where(lane_h0, x, 0.0), jnp.where(lane_h0, 0.0, x)], axis=0)

    def level_mask(m):
        return ((rowi // (2 * m)) == (lane // (2 * m))) & ((rowi % (2 * m)) >= m) & ((lane % (2 * m)) < m)

    level_masks = {m: level_mask(m) for m in (1, 2, 4, 8, 16, 32)}

    cums, cends = [], []
    for ic in range(nc):
        cum_c = jnp.dot(tri_incl, lw[ic * c:(ic + 1) * c, :], preferred_element_type=F32, precision=HIGHEST)
        cums.append(cum_c)
        cends.append(jnp.broadcast_to(cum_c[c - 1:c, :], (c, LANES)))
    cum = jnp.concatenate(cums, axis=0)
    cend = jnp.concatenate(cends, axis=0)
    einv = jnp.exp(-cum)
    eend = jnp.exp(cend - cum)
    a_til = kk * jnp.exp(cum - lw)
    r_til = r * jnp.exp(cum)
    k_til = kmod * einv
    b_til = bb * einv
    k_end = kmod * eend
    b_end = bb * eend

    chunks = range(nc)
    rows_of = lambda x, ic: x[ic * c:(ic + 1) * c]
    a_t = [bd(rows_of(a_til, ic)).astype(BF16) for ic in chunks]
    r_t = [bd(rows_of(r_til, ic)) for ic in chunks]
    k_t = [bd(rows_of(k_til, ic)).astype(BF16) for ic in chunks]
    b_t = [bd(rows_of(b_til, ic)).astype(BF16) for ic in chunks]
    k_e = [bd(rows_of(k_end, ic)).astype(BF16) for ic in chunks]
    b_e = [bd(rows_of(b_end, ic)).astype(BF16) for ic in chunks]
    v_b = [bd(rows_of(v, ic)).astype(BF16) for ic in chunks]
    w_end = [jnp.exp(cend[ic * c:ic * c + 1, :]) for ic in chunks]

    sc = [_dot_nt(jnp.concatenate([a_t[ic], r_t[ic].astype(BF16)], axis=0),
                  jnp.concatenate([k_t[ic], b_t[ic]], axis=0)) for ic in chunks]
    l_ab = [jnp.where(strict, s_[:LANES, LANES:], 0.0) for s_ in sc]
    a_ak = [jnp.where(strict, s_[:LANES, :LANES], 0.0).astype(BF16) for s_ in sc]
    a_rk = [jnp.where(incl, s_[LANES:, :LANES], 0.0).astype(BF16) for s_ in sc]
    a_rb = [jnp.where(incl, s_[LANES:, LANES:], 0.0).astype(BF16) for s_ in sc]

    x = [eye_f - jnp.where(level_masks[1], l_, 0.0) for l_ in l_ab]
    for m in (2, 4, 8, 16, 32):
        xc = [_inv_dot(x[ic], jnp.where(level_masks[m], l_ab[ic], 0.0)) for ic in chunks]
        x = [x[ic] - _inv_dot(xc[ic], x[ic]) for ic in chunks]

    akv = [_dot(a_ak[ic], v_b[ic]) for ic in chunks]
    rkv = [_dot(a_rk[ic], v_b[ic]) for ic in chunks]
    ktv = [_dot_tn(k_e[ic], v_b[ic]) for ic in chunks]
    ua = [_dot(x[ic].astype(BF16), jnp.concatenate([akv[ic].astype(BF16), a_t[ic]], axis=1)).astype(BF16)
          for ic in chunks]
    rb_ua = [_dot(a_rb[ic], ua[ic]) for ic in chunks]
    bt_ua = [_dot_tn(b_e[ic], ua[ic]) for ic in chunks]
    r_bars = [(r_t[ic] - rb_ua[ic][:, LANES:]).astype(BF16) for ic in chunks]
    y0 = [rkv[ic] - rb_ua[ic][:, :LANES] for ic in chunks]
    y0s = [y_[:c, :] + y_[c:, :] for y_ in y0]
    g_mats = [(jnp.where(eye, w_end[ic], 0.0) - bt_ua[ic][:, LANES:]).astype(BF16) for ic in chunks]
    h_mats = [ktv[ic] - bt_ua[ic][:, :LANES] for ic in chunks]

    p = p_ref[...]
    ys = []
    for ic in range(nc):
        pb = p.astype(BF16)
        y_bd = _dot(r_bars[ic], pb)
        ys.append(y_bd[:c, :] + y_bd[c:, :] + y0s[ic])
        p = _dot(g_mats[ic], pb) + h_mats[ic]
    p_ref[...] = p

    yv = jnp.concatenate(ys, axis=0)
    inv_d = 1.0 / RW_HEAD_DIM
    mean = head_sum(yv) * inv_d
    yc = yv - mean
    var = head_sum(yc * yc) * inv_d
    yn = yc * lax.rsqrt(var + RW_GN_EPS) * gng_ref[...] + gnb_ref[...]
    bonus = head_sum(r * kmod * rk_ref[...]) * v
    o_ref[0] = ((yn + bonus) * gate).astype(o_ref.dtype)


def _wkv(z_rw, mu, w0, w2, a0, a2, g2, k_k, k_a, r_k, gn_g, gn_b, ts):
    b, s, _ = z_rw.shape
    npair = RW_WIDTH // LANES
    lo_blk = (3 * RW_WIDTH) // (2 * LANES)
    row = lambda a: a.reshape(1, -1)

    def col(off):
        return pl.BlockSpec((1, ts, LANES), lambda ib, ip, it: (ib, it, off + ip))

    def vec(off):
        return pl.BlockSpec((1, LANES), lambda ib, ip, it: (0, off + ip))

    in_specs = [
        col(0), col(npair), col(2 * npair),
        pl.BlockSpec((1, ts, 2 * LANES), lambda ib, ip, it: (ib, it, lo_blk)),
        vec(0), vec(npair), vec(2 * npair),
        pl.BlockSpec((1, 2 * LANES), lambda ib, ip, it: (0, lo_blk)),
        vec(0), vec(0), vec(0), vec(0), vec(0), vec(0), vec(0),
        pl.BlockSpec((RW_DECAY_LORA, LANES), lambda ib, ip, it: (0, ip)),
        pl.BlockSpec((RW_ICLR_LORA, LANES), lambda ib, ip, it: (0, ip)),
        pl.BlockSpec((RW_GATE_LORA, LANES), lambda ib, ip, it: (0, ip)),
    ]
    scratch = [
        pltpu.VMEM((LANES, LANES), F32),
        pltpu.VMEM((1, LANES), F32), pltpu.VMEM((1, LANES), F32), pltpu.VMEM((1, LANES), F32),
        pltpu.VMEM((1, 2 * LANES), F32),
    ]
    mu2 = row(mu)
    return pl.pallas_call(
        _wkv_kernel,
        grid=(b, npair, s // ts),
        in_specs=in_specs,
        out_specs=pl.BlockSpec((1, ts, LANES), lambda ib, ip, it: (ib, it, ip)),
        out_shape=jax.ShapeDtypeStruct((b, s, RW_WIDTH), BF16),
        scratch_shapes=scratch,
        compiler_params=_cparams(("parallel", "parallel", "arbitrary")),
        name="wkv",
    )(z_rw, z_rw, z_rw, z_rw, mu2, mu2, mu2, mu2,
      row(w0), row(a0), row(k_k), row(k_a), row(r_k), row(gn_g), row(gn_b),
      w2.astype(BF16), a2.astype(BF16), g2.astype(BF16))


ATTN_ROWS = 16


def _attn_kernel(lq1_ref, lk1_ref, lq2_ref, lk2_ref, q_ref, k_ref, v_ref, subg_ref, o_ref,
                 qs_ref, s_ref, p_ref, m_ref, l_ref, alpha_ref, acc_ref, *, lambda_init, tk):
    tq = q_ref.shape[1]
    rows = 2 * tq
    qi = pl.program_id(2)
    ngroups = rows // ATTN_ROWS

    q = q_ref[0]
    lane_h0 = lax.broadcasted_iota(jnp.int32, q.shape, 1) < DA_QK_DIM
    zero = jnp.zeros_like(q)
    qsc = q * (DA_QK_DIM ** -0.5)
    qs_ref[...] = jnp.concatenate([jnp.where(lane_h0, qsc, zero), jnp.where(lane_h0, zero, qsc)], axis=0)
    m_ref[...] = jnp.full_like(m_ref, -jnp.inf)
    l_ref[...] = jnp.zeros_like(l_ref)
    acc_ref[...] = jnp.zeros_like(acc_ref)

    key_chunk = lax.broadcasted_iota(jnp.int32, (ATTN_ROWS, LANES), 1) // CHUNK

    def kv_block(j, masked):
        koff = pl.multiple_of(j * tk, tk)
        s_ref[...] = _dot_nt(qs_ref[...], k_ref[0, pl.ds(koff, tk), :])

        def load_scores(sl, r0, t):
            s = s_ref[sl, t * LANES:(t + 1) * LANES]
            if masked:
                q_chunk = (qi * tq + r0 % tq) // CHUNK - j * (tk // CHUNK) - t * (LANES // CHUNK)
                s = jnp.where(key_chunk <= q_chunk, s, -jnp.inf)
            return s

        def row_max(g, carry):
            r0 = pl.multiple_of(g * ATTN_ROWS, ATTN_ROWS)
            sl = pl.ds(r0, ATTN_ROWS)
            m_prev = m_ref[sl, :]
            tile_max = load_scores(sl, r0, 0)
            for t in range(1, tk // LANES):
                tile_max = jnp.maximum(tile_max, load_scores(sl, r0, t))
            m_new = jnp.maximum(m_prev, jnp.max(tile_max, axis=-1, keepdims=True))
            alpha_ref[sl, :] = jnp.exp(m_prev - m_new)
            m_ref[sl, :] = m_new
            return carry

        def row_exp(g, carry):
            r0 = pl.multiple_of(g * ATTN_ROWS, ATTN_ROWS)
            sl = pl.ds(r0, ATTN_ROWS)
            m_row = m_ref[sl, :]
            part = alpha_ref[sl, :] * l_ref[sl, :]
            for t in range(tk // LANES):
                p = jnp.exp(load_scores(sl, r0, t) - m_row)
                part = part + p
                p_ref[sl, t * LANES:(t + 1) * LANES] = p.astype(BF16)
            l_ref[sl, :] = part
            return carry

        lax.fori_loop(0, ngroups, row_max, 0, unroll=16)
        lax.fori_loop(0, ngroups, row_exp, 0, unroll=4)
        acc_ref[...] = alpha_ref[...] * acc_ref[...] + _dot(p_ref[...], v_ref[0, pl.ds(koff, tk), :])

    def run_blocks(lo, hi, masked):
        def body(j, carry):
            kv_block(j, masked)
            return carry
        lax.fori_loop(lo, hi, body, 0)

    nfull = (qi * tq + CHUNK) // tk
    nlast = (qi * tq + tq - 1) // tk + 1
    run_blocks(0, nfull, False)
    run_blocks(nfull, nlast, True)

    lam = (jnp.exp(jnp.sum(lq1_ref[...] * lk1_ref[...], axis=-1, keepdims=True))
           - jnp.exp(jnp.sum(lq2_ref[...] * lk2_ref[...], axis=-1, keepdims=True))
           + lambda_init)
    o = acc_ref[...] / jnp.sum(l_ref[...], axis=-1, keepdims=True)
    o = o[:tq, :] - lam * o[tq:, :]
    o = _rms(o, subg_ref[...], DA_SUBLN_EPS) * (1.0 - lambda_init)
    o_ref[0] = o.astype(o_ref.dtype)


def _attn(z_da, lq1, lk1, lq2, lk2, sub_g, lambda_init, tq, tk):
    b, s, _ = z_da.shape
    row = lambda a: a.reshape(1, -1)
    lam_spec = pl.BlockSpec((1, DA_QK_DIM), lambda ib, ih, iq: (0, 0))
    kern = functools.partial(_attn_kernel, lambda_init=lambda_init, tk=tk)
    rep = pltpu.VMEM((2 * tq, LANES), F32)
    return pl.pallas_call(
        kern,
        grid=(b, DA_HEADS, s // tq),
        in_specs=[
            lam_spec, lam_spec, lam_spec, lam_spec,
            pl.BlockSpec((1, tq, LANES), lambda ib, ih, iq: (ib, iq, ih)),
            pl.BlockSpec((1, s, LANES), lambda ib, ih, iq: (ib, 0, DA_HEADS + ih)),
            pl.BlockSpec((1, s, LANES), lambda ib, ih, iq: (ib, 0, 2 * DA_HEADS + ih)),
            pl.BlockSpec((1, DA_V_DIM), lambda ib, ih, iq: (0, 0)),
        ],
        out_specs=pl.BlockSpec((1, tq, LANES), lambda ib, ih, iq: (ib, iq, ih)),
        out_shape=jax.ShapeDtypeStruct((b, s, DA_V_WIDTH), BF16),
        scratch_shapes=[
            pltpu.VMEM((2 * tq, LANES), BF16),
            pltpu.VMEM((2 * tq, tk), F32),
            pltpu.VMEM((2 * tq, tk), BF16),
            rep, rep, rep,
            pltpu.VMEM((2 * tq, DA_V_DIM), F32),
        ],
        compiler_params=_cparams(("parallel", "parallel", "arbitrary")),
        name="attn",
    )(row(lq1), row(lk1), row(lq2), row(lk2), z_da, z_da, z_da, row(sub_g))


def _mix_kernel(x_ref, yrw_ref, yda_ref, grw_ref, gda_ref, wrw_ref, wda_ref, wo_ref, g_ref, o_ref):
    y_rw = _dot(yrw_ref[...], wrw_ref[...])
    y_da = _dot(yda_ref[...], wda_ref[...])
    merged = _sigmoid(grw_ref[...]) * y_rw + _sigmoid(gda_ref[...]) * y_da
    mo = _dot(merged.astype(BF16), wo_ref[...])
    o_ref[...] = x_ref[...] + _rms(mo, g_ref[...], RMS_EPS)


def _mix(x2d, y_rw, y_da, z_g, w_rw, w_da, w_o, g, tm):
    t, d = x2d.shape
    full = lambda shape: pl.BlockSpec(shape, lambda i: (0, 0))
    return pl.pallas_call(
        _mix_kernel,
        grid=(t // tm,),
        in_specs=[
            pl.BlockSpec((tm, d), lambda i: (i, 0)),
            pl.BlockSpec((tm, RW_WIDTH), lambda i: (i, 0)),
            pl.BlockSpec((tm, DA_V_WIDTH), lambda i: (i, 0)),
            pl.BlockSpec((tm, d), lambda i: (i, 0)),
            pl.BlockSpec((tm, d), lambda i: (i, 1)),
            full(w_rw.shape), full(w_da.shape), full(w_o.shape), full((1, d)),
        ],
        out_specs=pl.BlockSpec((tm, d), lambda i: (i, 0)),
        out_shape=jax.ShapeDtypeStruct((t, d), F32),
        compiler_params=_cparams(("parallel",)),
        name="mix",
    )(x2d, y_rw, y_da, z_g, z_g, w_rw, w_da, w_o, g)


def _mlp_kernel(x_ref, gpre_ref, wup_ref, wdn_ref, gpost_ref, o_ref, h_ref, acc_ref):
    j = pl.program_id(1)

    @pl.when(j == 0)
    def _():
        h_ref[...] = _rms(x_ref[...], gpre_ref[...], RMS_EPS).astype(BF16)
        acc_ref[...] = jnp.zeros_like(acc_ref)

    up = _dot(h_ref[...], wup_ref[...])
    act = jnp.square(jnp.maximum(up, 0.0))
    acc_ref[...] += _dot(act.astype(BF16), wdn_ref[...])

    @pl.when(j == pl.num_programs(1) - 1)
    def _():
        o_ref[...] = x_ref[...] + _rms(acc_ref[...], gpost_ref[...], RMS_EPS)


def _mlp(x2d, g_pre, w_up, w_dn, g_post, tm, tf):
    t, d = x2d.shape
    ff = w_up.shape[1]
    return pl.pallas_call(
        _mlp_kernel,
        grid=(t // tm, ff // tf),
        in_specs=[
            pl.BlockSpec((tm, d), lambda i, j: (i, 0)),
            pl.BlockSpec((1, d), lambda i, j: (0, 0)),
            pl.BlockSpec((d, tf), lambda i, j: (0, j)),
            pl.BlockSpec((tf, d), lambda i, j: (j, 0)),
            pl.BlockSpec((1, d), lambda i, j: (0, 0)),
        ],
        out_specs=pl.BlockSpec((tm, d), lambda i, j: (i, 0)),
        out_shape=jax.ShapeDtypeStruct((t, d), F32),
        scratch_shapes=[pltpu.VMEM((tm, d), BF16), pltpu.VMEM((tm, d), F32)],
        compiler_params=_cparams(("parallel", "arbitrary")),
        name="mlp",
    )(x2d, g_pre, w_up, w_dn, g_post)


def _tile(n, want):
    t = min(n, want)
    while n % t:
        t //= 2
    return t


def kernel(x, pre_mix_g, w_in, rw_shift_mu, rw_w0, rw_w2, rw_a0, rw_a2, rw_g2, rw_k_k, rw_k_a, rw_r_k, rw_gn_g, rw_gn_b, w_rw_out, da_lam_q1, da_lam_k1, da_lam_q2, da_lam_k2, da_sub_g, w_da_out, w_o, post_mix_g, pre_mlp_g, w_up, w_down, post_mlp_g):
    b, s, d = x.shape
    depth = w_in.shape[0]
    t = b * s
    tm = _tile(t, 512)
    for l in range(depth):
        lambda_init = 0.8 - 0.6 * math.exp(-0.3 * l)
        x2d = x.reshape(t, d)
        g_pre = pre_mix_g[l].reshape(1, d)
        w_in_l = w_in[l].astype(BF16)
        z_rw = _inproj(x2d, g_pre, w_in_l[:, :RW_IN], F32, tm, RW_IN // 2)
        z_da = _inproj(x2d, g_pre, w_in_l[:, RW_IN:RW_IN + DA_IN], BF16, tm, DA_IN // 2)
        z_g = _inproj(x2d, g_pre, w_in_l[:, RW_IN + DA_IN:], F32, tm, d)

        y_rw = _wkv(z_rw.reshape(b, s, RW_IN), rw_shift_mu[l], rw_w0[l], rw_w2[l], rw_a0[l],
                    rw_a2[l], rw_g2[l], rw_k_k[l], rw_k_a[l], rw_r_k[l], rw_gn_g[l], rw_gn_b[l],
                    _tile(s, 512))
        y_da = _attn(z_da.reshape(b, s, DA_IN), da_lam_q1[l], da_lam_k1[l], da_lam_q2[l],
                     da_lam_k2[l], da_sub_g[l], lambda_init, _tile(s, 512), _tile(s, 1024))

        x1 = _mix(x2d, y_rw.reshape(t, RW_WIDTH), y_da.reshape(t, DA_V_WIDTH), z_g,
                  w_rw_out[l].astype(BF16), w_da_out[l].astype(BF16), w_o[l].astype(BF16),
                  post_mix_g[l].reshape(1, d), tm)
        x2 = _mlp(x1, pre_mlp_g[l].reshape(1, d), w_up[l].astype(BF16), w_down[l].astype(BF16),
                  post_mlp_g[l].reshape(1, d), _tile(t, 1024), 1024)
        x = x2.reshape(b, s, d)
    return x
```

```python
import functools
import math

import jax
import jax.numpy as jnp
from jax import lax
from jax.experimental import pallas as pl
from jax.experimental.pallas import tpu as pltpu

F32 = jnp.float32
BF16 = jnp.bfloat16

RMS_EPS = 1e-6
CHUNK = 64
RW_HEADS = 8
RW_HEAD_DIM = 64
RW_WIDTH = RW_HEADS * RW_HEAD_DIM
RW_DECAY_LORA = 64
RW_ICLR_LORA = 64
RW_GATE_LORA = 128
RW_GN_EPS = 1e-5 * RW_HEAD_DIM
RW_IN = 3 * RW_WIDTH + RW_DECAY_LORA + RW_ICLR_LORA + RW_GATE_LORA
DA_HEADS = 4
DA_QK_DIM = 64
DA_V_DIM = 2 * DA_QK_DIM
DA_QK_WIDTH = DA_HEADS * 2 * DA_QK_DIM
DA_V_WIDTH = DA_HEADS * DA_V_DIM
DA_IN = 2 * DA_QK_WIDTH + DA_V_WIDTH
DA_SUBLN_EPS = 1e-5

LANES = 128
WKV_CHUNK = 64
VMEM_LIMIT = 56 * 1024 * 1024

HIGHEST = lax.Precision.HIGHEST


def _cparams(semantics):
    return pltpu.CompilerParams(dimension_semantics=semantics, vmem_limit_bytes=VMEM_LIMIT)


def _rms(x, g, eps):
    return x * lax.rsqrt(jnp.mean(x * x, axis=-1, keepdims=True) + eps) * g


def _dot(a, b):
    return jnp.dot(a, b, preferred_element_type=F32)


def _dot_nt(a, b):
    return lax.dot_general(a, b, (((1,), (1,)), ((), ())), preferred_element_type=F32)


def _dot_tn(a, b):
    return lax.dot_general(a, b, (((0,), (0,)), ((), ())), preferred_element_type=F32)


def _inproj_kernel(x_ref, g_ref, w_ref, o_ref):
    h = _rms(x_ref[...], g_ref[...], RMS_EPS)
    o_ref[...] = _dot(h.astype(BF16), w_ref[...]).astype(o_ref.dtype)


def _inproj(x2d, g, w_bf16, out_dtype, tm, tn):
    t, d = x2d.shape
    n = w_bf16.shape[1]
    return pl.pallas_call(
        _inproj_kernel,
        grid=(n // tn, t // tm),
        in_specs=[
            pl.BlockSpec((tm, d), lambda j, i: (i, 0)),
            pl.BlockSpec((1, d), lambda j, i: (0, 0)),
            pl.BlockSpec((d, tn), lambda j, i: (0, j)),
        ],
        out_specs=pl.BlockSpec((tm, tn), lambda j, i: (i, j)),
        out_shape=jax.ShapeDtypeStruct((t, n), out_dtype),
        compiler_params=_cparams(("parallel", "parallel")),
        name="inproj",
    )(x2d, g, w_bf16)


def _sigmoid(x):
    return 1.0 / (1.0 + jnp.exp(-x))


def _inv_dot(a, b):
    return _dot(a.astype(BF16), b.astype(BF16))


def _wkv_kernel(r_ref, k_ref, v_ref, lo_ref, mur_ref, muk_ref, muv_ref, mulo_ref,
                w0_ref, a0_ref, kk_ref, ka_ref, rk_ref, gng_ref, gnb_ref,
                w2_ref, a2_ref, g2_ref, o_ref,
                p_ref, cr_ref, ck_ref, cv_ref, clo_ref):
    ts = r_ref.shape[1]
    nc = ts // WKV_CHUNK
    c = WKV_CHUNK

    @pl.when(pl.program_id(2) == 0)
    def _():
        p_ref[...] = jnp.zeros_like(p_ref)
        cr_ref[...] = jnp.zeros_like(cr_ref)
        ck_ref[...] = jnp.zeros_like(ck_ref)
        cv_ref[...] = jnp.zeros_like(cv_ref)
        clo_ref[...] = jnp.zeros_like(clo_ref)

    def shifted(z, carry_ref, mu):
        prev = pltpu.roll(z, 1, axis=0)
        row = lax.broadcasted_iota(jnp.int32, z.shape, 0)
        prev = jnp.where(row == 0, carry_ref[...], prev)
        carry_ref[...] = z[ts - 1:ts, :]
        return z + (prev - z) * mu

    r = shifted(r_ref[0], cr_ref, mur_ref[...])
    k = shifted(k_ref[0], ck_ref, muk_ref[...])
    v = shifted(v_ref[0], cv_ref, muv_ref[...])
    lo = shifted(lo_ref[0], clo_ref, mulo_ref[...])
    xwa = lo[:, :LANES]
    xg = lo[:, LANES:]

    zeros64 = jnp.zeros((RW_DECAY_LORA, LANES), BF16)
    w2p = jnp.concatenate([w2_ref[...], zeros64], axis=0)
    a2p = jnp.concatenate([zeros64, a2_ref[...]], axis=0)
    wl = _dot(jnp.tanh(xwa).astype(BF16), w2p)
    al = _dot(xwa.astype(BF16), a2p)
    gate = _dot(_sigmoid(xg).astype(BF16), g2_ref[...])

    y = -(w0_ref[...] + wl)
    softplus = jnp.maximum(y, 0.0) + jnp.log(1.0 + jnp.exp(-jnp.abs(y)))
    lw = -jnp.exp(-softplus - 0.5)
    a = _sigmoid(a0_ref[...] + al)

    lane = lax.broadcasted_iota(jnp.int32, (LANES, LANES), 1)
    rowi = lax.broadcasted_iota(jnp.int32, (LANES, LANES), 0)
    same_head = (lane // RW_HEAD_DIM) == (rowi // RW_HEAD_DIM)
    head_ones = jnp.where(same_head, 1.0, 0.0).astype(F32)

    def head_sum(x):
        return jnp.dot(x, head_ones, preferred_element_type=F32, precision=HIGHEST)

    kk = k * kk_ref[...]
    kk = kk / jnp.maximum(jnp.sqrt(head_sum(kk * kk)), 1e-12)
    kmod = k * (1.0 + (a - 1.0) * ka_ref[...])

    bb = kk * a

    ri = lax.broadcasted_iota(jnp.int32, (c, c), 0)
    ci = lax.broadcasted_iota(jnp.int32, (c, c), 1)
    tri_incl = jnp.where(ci <= ri, 1.0, 0.0).astype(F32)
    lane_h0 = lax.broadcasted_iota(jnp.int32, (c, LANES), 1) < RW_HEAD_DIM
    strict = lane < rowi
    incl = lane <= rowi
    eye = lane == rowi
    eye_f = jnp.where(eye, 1.0, 0.0)

    def bd(x):
        return jnp.concatenate([jnp.where(lane_h0, x, 0.0), jnp.where(lane_h0, 0.0, x)], axis=0)

    def level_mask(m):
        return ((rowi // (2 * m)) == (lane // (2 * m))) & ((rowi % (2 * m)) >= m) & ((lane % (2 * m)) < m)

    level_masks = {m: level_mask(m) for m in (1, 2, 4, 8, 16, 32)}

    cums, cends = [], []
    for ic in range(nc):
        cum_c = jnp.dot(tri_incl, lw[ic * c:(ic + 1) * c, :], preferred_element_type=F32, precision=HIGHEST)
        cums.append(cum_c)
        cends.append(jnp.broadcast_to(cum_c[c - 1:c, :], (c, LANES)))
    cum = jnp.concatenate(cums, axis=0)
    cend = jnp.concatenate(cends, axis=0)
    einv = jnp.exp(-cum)
    eend = jnp.exp(cend - cum)
    a_til = kk * jnp.exp(cum - lw)
    r_til = r * jnp.exp(cum)
    k_til = kmod * einv
    b_til = bb * einv
    k_end = kmod * eend
    b_end = bb * eend

    chunks = range(nc)
    rows_of = lambda x, ic: x[ic * c:(ic + 1) * c]
    a_t = [bd(rows_of(a_til, ic)).astype(BF16) for ic in chunks]
    r_t = [bd(rows_of(r_til, ic)) for ic in chunks]
    k_t = [bd(rows_of(k_til, ic)).astype(BF16) for ic in chunks]
    b_t = [bd(rows_of(b_til, ic)).astype(BF16) for ic in chunks]
    k_e = [bd(rows_of(k_end, ic)).astype(BF16) for ic in chunks]
    b_e = [bd(rows_of(b_end, ic)).astype(BF16) for ic in chunks]
    v_b = [bd(rows_of(v, ic)).astype(BF16) for ic in chunks]
    w_end = [jnp.exp(cend[ic * c:ic * c + 1, :]) for ic in chunks]

    sc = [_dot_nt(jnp.concatenate([a_t[ic], r_t[ic].astype(BF16)], axis=0),
                  jnp.concatenate([k_t[ic], b_t[ic]], axis=0)) for ic in chunks]
    l_ab = [jnp.where(strict, s_[:LANES, LANES:], 0.0) for s_ in sc]
    a_ak = [jnp.where(strict, s_[:LANES, :LANES], 0.0).astype(BF16) for s_ in sc]
    a_rk = [jnp.where(incl, s_[LANES:, :LANES], 0.0).astype(BF16) for s_ in sc]
    a_rb = [jnp.where(incl, s_[LANES:, LANES:], 0.0).astype(BF16) for s_ in sc]

    x = [eye_f - jnp.where(level_masks[1], l_, 0.0) for l_ in l_ab]
    for m in (2, 4, 8, 16, 32):
        xc = [_inv_dot(x[ic], jnp.where(level_masks[m], l_ab[ic], 0.0)) for ic in chunks]
        x = [x[ic] - _inv_dot(xc[ic], x[ic]) for ic in chunks]

    akv = [_dot(a_ak[ic], v_b[ic]) for ic in chunks]
    rkv = [_dot(a_rk[ic], v_b[ic]) for ic in chunks]
    ktv = [_dot_tn(k_e[ic], v_b[ic]) for ic in chunks]
    ua = [_dot(x[ic].astype(BF16), jnp.concatenate([akv[ic].astype(BF16), a_t[ic]], axis=1)).astype(BF16)
          for ic in chunks]
    rb_ua = [_dot(a_rb[ic], ua[ic]) for ic in chunks]
    bt_ua = [_dot_tn(b_e[ic], ua[ic]) for ic in chunks]
    r_bars = [(r_t[ic] - rb_ua[ic][:, LANES:]).astype(BF16) for ic in chunks]
    y0 = [rkv[ic] - rb_ua[ic][:, :LANES] for ic in chunks]
    y0s = [y_[:c, :] + y_[c:, :] for y_ in y0]
    g_mats = [(jnp.where(eye, w_end[ic], 0.0) - bt_ua[ic][:, LANES:]).astype(BF16) for ic in chunks]
    h_mats = [ktv[ic] - bt_ua[ic][:, :LANES] for ic in chunks]

    p = p_ref[...]
    ys = []
    for ic in range(nc):
        pb = p.astype(BF16)
        y_bd = _dot(r_bars[ic], pb)
        ys.append(y_bd[:c, :] + y_bd[c:, :] + y0s[ic])
        p = _dot(g_mats[ic], pb) + h_mats[ic]
    p_ref[...] = p

    yv = jnp.concatenate(ys, axis=0)
    inv_d = 1.0 / RW_HEAD_DIM
    mean = head_sum(yv) * inv_d
    yc = yv - mean
    var = head_sum(yc * yc) * inv_d
    yn = yc * lax.rsqrt(var + RW_GN_EPS) * gng_ref[...] + gnb_ref[...]
    bonus = head_sum(r * kmod * rk_ref[...]) * v
    o_ref[0] = ((yn + bonus) * gate).astype(o_ref.dtype)


def _wkv(z_rw, mu, w0, w2, a0, a2, g2, k_k, k_a, r_k, gn_g, gn_b, ts):
    b, s, _ = z_rw.shape
    npair = RW_WIDTH // LANES
    lo_blk = (3 * RW_WIDTH) // (2 * LANES)
    row = lambda a: a.reshape(1, -1)

    def col(off):
        return pl.BlockSpec((1, ts, LANES), lambda ib, ip, it: (ib, it, off + ip))

    def vec(off):
        return pl.BlockSpec((1, LANES), lambda ib, ip, it: (0, off + ip))

    in_specs = [
        col(0), col(npair), col(2 * npair),
        pl.BlockSpec((1, ts, 2 * LANES), lambda ib, ip, it: (ib, it, lo_blk)),
        vec(0), vec(npair), vec(2 * npair),
        pl.BlockSpec((1, 2 * LANES), lambda ib, ip, it: (0, lo_blk)),
        vec(0), vec(0), vec(0), vec(0), vec(0), vec(0), vec(0),
        pl.BlockSpec((RW_DECAY_LORA, LANES), lambda ib, ip, it: (0, ip)),
        pl.BlockSpec((RW_ICLR_LORA, LANES), lambda ib, ip, it: (0, ip)),
        pl.BlockSpec((RW_GATE_LORA, LANES), lambda ib, ip, it: (0, ip)),
    ]
    scratch = [
        pltpu.VMEM((LANES, LANES), F32),
        pltpu.VMEM((1, LANES), F32), pltpu.VMEM((1, LANES), F32), pltpu.VMEM((1, LANES), F32),
        pltpu.VMEM((1, 2 * LANES), F32),
    ]
    mu2 = row(mu)
    return pl.pallas_call(
        _wkv_kernel,
        grid=(b, npair, s // ts),
        in_specs=in_specs,
        out_specs=pl.BlockSpec((1, ts, LANES), lambda ib, ip, it: (ib, it, ip)),
        out_shape=jax.ShapeDtypeStruct((b, s, RW_WIDTH), BF16),
        scratch_shapes=scratch,
        compiler_params=_cparams(("parallel", "parallel", "arbitrary")),
        name="wkv",
    )(z_rw, z_rw, z_rw, z_rw, mu2, mu2, mu2, mu2,
      row(w0), row(a0), row(k_k), row(k_a), row(r_k), row(gn_g), row(gn_b),
      w2.astype(BF16), a2.astype(BF16), g2.astype(BF16))


ATTN_ROWS = 16


def _attn_kernel(lq1_ref, lk1_ref, lq2_ref, lk2_ref, q_ref, k_ref, v_ref, subg_ref, o_ref,
                 qs_ref, s_ref, p_ref, m_ref, alpha_ref, acc_ref, *, lambda_init, tk):
    tq = q_ref.shape[1]
    rows = 2 * tq
    qi = pl.program_id(2)
    ngroups = rows // ATTN_ROWS
    ntiles = tk // LANES
    heads = (0, 1)

    lane_h0 = lax.broadcasted_iota(jnp.int32, (tq, LANES), 1) < DA_QK_DIM
    for h in heads:
        q = q_ref[0, :, h * LANES:(h + 1) * LANES]
        zero = jnp.zeros_like(q)
        qsc = q * (DA_QK_DIM ** -0.5)
        qs_ref[h] = jnp.concatenate([jnp.where(lane_h0, qsc, zero), jnp.where(lane_h0, zero, qsc)], axis=0)
    m_ref[...] = jnp.full_like(m_ref, -jnp.inf)
    alpha_ref[...] = jnp.zeros_like(alpha_ref)
    acc_ref[...] = jnp.zeros_like(acc_ref)
    p_ref[1] = jnp.zeros((rows, tk), BF16)

    key_chunk = lax.broadcasted_iota(jnp.int32, (ATTN_ROWS, LANES), 1) // CHUNK
    ones = jnp.ones((tk, LANES), BF16)

    def scores(h, j):
        koff = pl.multiple_of(j * tk, tk)
        s_ref[h] = _dot_nt(qs_ref[h], k_ref[0, pl.ds(koff, tk), h * LANES:(h + 1) * LANES])

    def accumulate(h, j):
        koff = pl.multiple_of(j * tk, tk)
        v_ext = jnp.concatenate([v_ref[0, pl.ds(koff, tk), h * LANES:(h + 1) * LANES], ones], axis=1)
        alpha = alpha_ref[h]
        acc_ref[h] = jnp.concatenate([alpha, alpha], axis=1) * acc_ref[h] + _dot(p_ref[h], v_ext)

    def softmax(h, j, masked):
        def load_scores(g, t):
            r0 = g * ATTN_ROWS
            s = s_ref[h, r0:r0 + ATTN_ROWS, t * LANES:(t + 1) * LANES]
            if masked:
                q_chunk = (qi * tq + r0 % tq) // CHUNK - j * (tk // CHUNK) - t * (LANES // CHUNK)
                s = jnp.where(key_chunk <= q_chunk, s, -jnp.inf)
            return s

        for g in range(ngroups):
            sl = slice(g * ATTN_ROWS, (g + 1) * ATTN_ROWS)
            m_prev = m_ref[h, sl, :]
            tile_max = load_scores(g, 0)
            for t in range(1, ntiles):
                tile_max = jnp.maximum(tile_max, load_scores(g, t))
            m_new = jnp.maximum(m_prev, jnp.max(tile_max, axis=-1, keepdims=True))
            alpha_ref[h, sl, :] = jnp.exp(m_prev - m_new)
            m_ref[h, sl, :] = m_new
        for g in range(ngroups):
            sl = slice(g * ATTN_ROWS, (g + 1) * ATTN_ROWS)
            m_row = m_ref[h, sl, :]
            for t in range(ntiles):
                p_ref[h, sl, t * LANES:(t + 1) * LANES] = jnp.exp(load_scores(g, t) - m_row).astype(BF16)

    def step(j, masked):
        softmax(0, j, masked)
        accumulate(1, jnp.maximum(j - 1, 0))
        scores(1, j)
        softmax(1, j, masked)
        accumulate(0, j)
        if not masked:
            scores(0, j + 1)

    nfull = (qi * tq + CHUNK) // tk
    scores(0, 0)

    def full_step(j, carry):
        step(j, False)
        return carry

    lax.fori_loop(0, nfull, full_step, 0)
    step(nfull, True)
    accumulate(1, nfull)

    lam = (jnp.exp(jnp.sum(lq1_ref[...] * lk1_ref[...], axis=-1, keepdims=True))
           - jnp.exp(jnp.sum(lq2_ref[...] * lk2_ref[...], axis=-1, keepdims=True))
           + lambda_init)
    for h in heads:
        acc = acc_ref[h]
        o = acc[:, :DA_V_DIM] / acc[:, DA_V_DIM:]
        o = o[:tq, :] - lam * o[tq:, :]
        o = _rms(o, subg_ref[...], DA_SUBLN_EPS) * (1.0 - lambda_init)
        o_ref[0, :, h * LANES:(h + 1) * LANES] = o.astype(o_ref.dtype)


def _attn(z_da, lq1, lk1, lq2, lk2, sub_g, lambda_init, tq, tk):
    b, s, _ = z_da.shape
    assert tk % tq == 0 and s % tk == 0
    row = lambda a: a.reshape(1, -1)
    lam_spec = pl.BlockSpec((1, DA_QK_DIM), lambda ib, ih, iq: (0, 0))
    kern = functools.partial(_attn_kernel, lambda_init=lambda_init, tk=tk)
    rows = 2 * tq
    pair = 2 * LANES
    npairs = DA_HEADS // 2
    rep = pltpu.VMEM((2, rows, LANES), F32)
    return pl.pallas_call(
        kern,
        grid=(b, npairs, s // tq),
        in_specs=[
            lam_spec, lam_spec, lam_spec, lam_spec,
            pl.BlockSpec((1, tq, pair), lambda ib, ih, iq: (ib, iq, ih)),
            pl.BlockSpec((1, s, pair), lambda ib, ih, iq: (ib, 0, npairs + ih)),
            pl.BlockSpec((1, s, pair), lambda ib, ih, iq: (ib, 0, 2 * npairs + ih)),
            pl.BlockSpec((1, DA_V_DIM), lambda ib, ih, iq: (0, 0)),
        ],
        out_specs=pl.BlockSpec((1, tq, pair), lambda ib, ih, iq: (ib, iq, ih)),
        out_shape=jax.ShapeDtypeStruct((b, s, DA_V_WIDTH), BF16),
        scratch_shapes=[
            pltpu.VMEM((2, rows, LANES), BF16),
            pltpu.VMEM((2, rows, tk), F32),
            pltpu.VMEM((2, rows, tk), BF16),
            rep, rep,
            pltpu.VMEM((2, rows, 2 * DA_V_DIM), F32),
        ],
        compiler_params=_cparams(("parallel", "parallel", "arbitrary")),
        name="attn",
    )(row(lq1), row(lk1), row(lq2), row(lk2), z_da, z_da, z_da, row(sub_g))


def _mix_kernel(x_ref, yrw_ref, yda_ref, grw_ref, gda_ref, wrw_ref, wda_ref, wo_ref, g_ref, o_ref):
    y_rw = _dot(yrw_ref[...], wrw_ref[...])
    y_da = _dot(yda_ref[...], wda_ref[...])
    merged = _sigmoid(grw_ref[...]) * y_rw + _sigmoid(gda_ref[...]) * y_da
    mo = _dot(merged.astype(BF16), wo_ref[...])
    o_ref[...] = x_ref[...] + _rms(mo, g_ref[...], RMS_EPS)


def _mix(x2d, y_rw, y_da, z_g, w_rw, w_da, w_o, g, tm):
    t, d = x2d.shape
    full = lambda shape: pl.BlockSpec(shape, lambda i: (0, 0))
    return pl.pallas_call(
        _mix_kernel,
        grid=(t // tm,),
        in_specs=[
            pl.BlockSpec((tm, d), lambda i: (i, 0)),
            pl.BlockSpec((tm, RW_WIDTH), lambda i: (i, 0)),
            pl.BlockSpec((tm, DA_V_WIDTH), lambda i: (i, 0)),
            pl.BlockSpec((tm, d), lambda i: (i, 0)),
            pl.BlockSpec((tm, d), lambda i: (i, 1)),
            full(w_rw.shape), full(w_da.shape), full(w_o.shape), full((1, d)),
        ],
        out_specs=pl.BlockSpec((tm, d), lambda i: (i, 0)),
        out_shape=jax.ShapeDtypeStruct((t, d), F32),
        compiler_params=_cparams(("parallel",)),
        name="mix",
    )(x2d, y_rw, y_da, z_g, z_g, w_rw, w_da, w_o, g)


def _mlp_kernel(x_ref, gpre_ref, wup_ref, wdn_ref, gpost_ref, o_ref, h_ref, acc_ref):
    j = pl.program_id(1)

    @pl.when(j == 0)
    def _():
        h_ref[...] = _rms(x_ref[...], gpre_ref[...], RMS_EPS).astype(BF16)
        acc_ref[...] = jnp.zeros_like(acc_ref)

    up = _dot(h_ref[...], wup_ref[...])
    act = jnp.square(jnp.maximum(up, 0.0))
    acc_ref[...] += _dot(act.astype(BF16), wdn_ref[...])

    @pl.when(j == pl.num_programs(1) - 1)
    def _():
        o_ref[...] = x_ref[...] + _rms(acc_ref[...], gpost_ref[...], RMS_EPS)


def _mlp(x2d, g_pre, w_up, w_dn, g_post, tm, tf):
    t, d = x2d.shape
    ff = w_up.shape[1]
    return pl.pallas_call(
        _mlp_kernel,
        grid=(t // tm, ff // tf),
        in_specs=[
            pl.BlockSpec((tm, d), lambda i, j: (i, 0)),
            pl.BlockSpec((1, d), lambda i, j: (0, 0)),
            pl.BlockSpec((d, tf), lambda i, j: (0, j)),
            pl.BlockSpec((tf, d), lambda i, j: (j, 0)),
            pl.BlockSpec((1, d), lambda i, j: (0, 0)),
        ],
        out_specs=pl.BlockSpec((tm, d), lambda i, j: (i, 0)),
        out_shape=jax.ShapeDtypeStruct((t, d), F32),
        scratch_shapes=[pltpu.VMEM((tm, d), BF16), pltpu.VMEM((tm, d), F32)],
        compiler_params=_cparams(("parallel", "arbitrary")),
        name="mlp",
    )(x2d, g_pre, w_up, w_dn, g_post)


def _tile(n, want):
    t = min(n, want)
    while n % t:
        t //= 2
    return t


def kernel(x, pre_mix_g, w_in, rw_shift_mu, rw_w0, rw_w2, rw_a0, rw_a2, rw_g2, rw_k_k, rw_k_a, rw_r_k, rw_gn_g, rw_gn_b, w_rw_out, da_lam_q1, da_lam_k1, da_lam_q2, da_lam_k2, da_sub_g, w_da_out, w_o, post_mix_g, pre_mlp_g, w_up, w_down, post_mlp_g):
    b, s, d = x.shape
    depth = w_in.shape[0]
    t = b * s
    tm = _tile(t, 512)
    for l in range(depth):
        lambda_init = 0.8 - 0.6 * math.exp(-0.3 * l)
        x2d = x.reshape(t, d)
        g_pre = pre_mix_g[l].reshape(1, d)
        w_in_l = w_in[l].astype(BF16)
        z_rw = _inproj(x2d, g_pre, w_in_l[:, :RW_IN], F32, tm, RW_IN // 2)
        z_da = _inproj(x2d, g_pre, w_in_l[:, RW_IN:RW_IN + DA_IN], BF16, tm, DA_IN // 2)
        z_g = _inproj(x2d, g_pre, w_in_l[:, RW_IN + DA_IN:], F32, tm, d)

        y_rw = _wkv(z_rw.reshape(b, s, RW_IN), rw_shift_mu[l], rw_w0[l], rw_w2[l], rw_a0[l],
                    rw_a2[l], rw_g2[l], rw_k_k[l], rw_k_a[l], rw_r_k[l], rw_gn_g[l], rw_gn_b[l],
                    _tile(s, 512))
        y_da = _attn(z_da.reshape(b, s, DA_IN), da_lam_q1[l], da_lam_k1[l], da_lam_q2[l],
                     da_lam_k2[l], da_sub_g[l], lambda_init, _tile(s, 512), _tile(s, 1024))

        x1 = _mix(x2d, y_rw.reshape(t, RW_WIDTH), y_da.reshape(t, DA_V_WIDTH), z_g,
                  w_rw_out[l].astype(BF16), w_da_out[l].astype(BF16), w_o[l].astype(BF16),
                  post_mix_g[l].reshape(1, d), tm)
        x2 = _mlp(x1, pre_mlp_g[l].reshape(1, d), w_up[l].astype(BF16), w_down[l].astype(BF16),
                  post_mlp_g[l].reshape(1, d), _tile(t, 1024), 1024)
        x = x2.reshape(b, s, d)
    return x
```

```python
import functools
import math

import jax
import jax.numpy as jnp
from jax import lax
from jax.experimental import pallas as pl
from jax.experimental.pallas import tpu as pltpu

F32 = jnp.float32
BF16 = jnp.bfloat16

RMS_EPS = 1e-6
CHUNK = 64
RW_HEADS = 8
RW_HEAD_DIM = 64
RW_WIDTH = RW_HEADS * RW_HEAD_DIM
RW_DECAY_LORA = 64
RW_ICLR_LORA = 64
RW_GATE_LORA = 128
RW_GN_EPS = 1e-5 * RW_HEAD_DIM
RW_IN = 3 * RW_WIDTH + RW_DECAY_LORA + RW_ICLR_LORA + RW_GATE_LORA
DA_HEADS = 4
DA_QK_DIM = 64
DA_V_DIM = 2 * DA_QK_DIM
DA_QK_WIDTH = DA_HEADS * 2 * DA_QK_DIM
DA_V_WIDTH = DA_HEADS * DA_V_DIM
DA_IN = 2 * DA_QK_WIDTH + DA_V_WIDTH
DA_SUBLN_EPS = 1e-5

LANES = 128
WKV_CHUNK = 64
VMEM_LIMIT = 56 * 1024 * 1024

HIGHEST = lax.Precision.HIGHEST


def _cparams(semantics):
    return pltpu.CompilerParams(dimension_semantics=semantics, vmem_limit_bytes=VMEM_LIMIT)


def _rms(x, g, eps):
    return x * lax.rsqrt(jnp.mean(x * x, axis=-1, keepdims=True) + eps) * g


def _dot(a, b):
    return jnp.dot(a, b, preferred_element_type=F32)


def _dot_nt(a, b):
    return lax.dot_general(a, b, (((1,), (1,)), ((), ())), preferred_element_type=F32)


def _dot_tn(a, b):
    return lax.dot_general(a, b, (((0,), (0,)), ((), ())), preferred_element_type=F32)


def _inproj_kernel(x_ref, g_ref, w_ref, orw_ref, oda_ref, og_ref):
    h = _rms(x_ref[...], g_ref[...], RMS_EPS).astype(BF16)
    o1, o2 = RW_IN, RW_IN + DA_IN
    orw_ref[...] = _dot(h, w_ref[:, :o1])
    oda_ref[...] = _dot(h, w_ref[:, o1:o2]).astype(oda_ref.dtype)
    og_ref[...] = _dot(h, w_ref[:, o2:])


def _inproj(x2d, g, w_bf16, tm):
    t, d = x2d.shape
    n = w_bf16.shape[1]
    n_gate = n - RW_IN - DA_IN
    return pl.pallas_call(
        _inproj_kernel,
        grid=(t // tm,),
        in_specs=[
            pl.BlockSpec((tm, d), lambda i: (i, 0)),
            pl.BlockSpec((1, d), lambda i: (0, 0)),
            pl.BlockSpec((d, n), lambda i: (0, 0)),
        ],
        out_specs=[
            pl.BlockSpec((tm, RW_IN), lambda i: (i, 0)),
            pl.BlockSpec((tm, DA_IN), lambda i: (i, 0)),
            pl.BlockSpec((tm, n_gate), lambda i: (i, 0)),
        ],
        out_shape=[
            jax.ShapeDtypeStruct((t, RW_IN), F32),
            jax.ShapeDtypeStruct((t, DA_IN), BF16),
            jax.ShapeDtypeStruct((t, n_gate), F32),
        ],
        compiler_params=_cparams(("parallel",)),
        name="inproj",
    )(x2d, g, w_bf16)


def _sigmoid(x):
    return 1.0 / (1.0 + jnp.exp(-x))


def _inv_dot(a, b):
    return _dot(a.astype(BF16), b.astype(BF16))


def _wkv_kernel(r_ref, k_ref, v_ref, lo_ref, mur_ref, muk_ref, muv_ref, mulo_ref,
                w0_ref, a0_ref, kk_ref, ka_ref, rk_ref, gng_ref, gnb_ref,
                w2_ref, a2_ref, g2_ref, o_ref,
                p_ref, cr_ref, ck_ref, cv_ref, clo_ref):
    ts = r_ref.shape[1]
    nc = ts // WKV_CHUNK
    c = WKV_CHUNK

    @pl.when(pl.program_id(2) == 0)
    def _():
        p_ref[...] = jnp.zeros_like(p_ref)
        cr_ref[...] = jnp.zeros_like(cr_ref)
        ck_ref[...] = jnp.zeros_like(ck_ref)
        cv_ref[...] = jnp.zeros_like(cv_ref)
        clo_ref[...] = jnp.zeros_like(clo_ref)

    def shifted(z, carry_ref, mu):
        prev = pltpu.roll(z, 1, axis=0)
        row = lax.broadcasted_iota(jnp.int32, z.shape, 0)
        prev = jnp.where(row == 0, carry_ref[...], prev)
        carry_ref[...] = z[ts - 1:ts, :]
        return z + (prev - z) * mu

    r = shifted(r_ref[0], cr_ref, mur_ref[...])
    k = shifted(k_ref[0], ck_ref, muk_ref[...])
    v = shifted(v_ref[0], cv_ref, muv_ref[...])
    lo = shifted(lo_ref[0], clo_ref, mulo_ref[...])
    xwa = lo[:, :LANES]
    xg = lo[:, LANES:]

    zeros64 = jnp.zeros((RW_DECAY_LORA, LANES), BF16)
    w2p = jnp.concatenate([w2_ref[...], zeros64], axis=0)
    a2p = jnp.concatenate([zeros64, a2_ref[...]], axis=0)
    wl = _dot(jnp.tanh(xwa).astype(BF16), w2p)
    al = _dot(xwa.astype(BF16), a2p)
    gate = _dot(_sigmoid(xg).astype(BF16), g2_ref[...])

    y = -(w0_ref[...] + wl)
    softplus = jnp.maximum(y, 0.0) + jnp.log(1.0 + jnp.exp(-jnp.abs(y)))
    lw = -jnp.exp(-softplus - 0.5)
    a = _sigmoid(a0_ref[...] + al)

    lane = lax.broadcasted_iota(jnp.int32, (LANES, LANES), 1)
    rowi = lax.broadcasted_iota(jnp.int32, (LANES, LANES), 0)
    same_head = (lane // RW_HEAD_DIM) == (rowi // RW_HEAD_DIM)
    head_ones = jnp.where(same_head, 1.0, 0.0).astype(F32)

    def head_sum(x):
        return jnp.dot(x, head_ones, preferred_element_type=F32, precision=HIGHEST)

    kk = k * kk_ref[...]
    kk = kk / jnp.maximum(jnp.sqrt(head_sum(kk * kk)), 1e-12)
    kmod = k * (1.0 + (a - 1.0) * ka_ref[...])

    bb = kk * a

    ri = lax.broadcasted_iota(jnp.int32, (c, c), 0)
    ci = lax.broadcasted_iota(jnp.int32, (c, c), 1)
    tri_incl = jnp.where(ci <= ri, 1.0, 0.0).astype(F32)
    lane_h0 = lax.broadcasted_iota(jnp.int32, (c, LANES), 1) < RW_HEAD_DIM
    strict = lane < rowi
    incl = lane <= rowi
    eye = lane == rowi
    eye_f = jnp.where(eye, 1.0, 0.0)

    def bd(x):
        return jnp.concatenate([jnp.where(lane_h0, x, 0.0), jnp.where(lane_h0, 0.0, x)], axis=0)

    def level_mask(m):
        return ((rowi // (2 * m)) == (lane // (2 * m))) & ((rowi % (2 * m)) >= m) & ((lane % (2 * m)) < m)

    level_masks = {m: level_mask(m) for m in (1, 2, 4, 8, 16, 32)}

    cums, cends = [], []
    for ic in range(nc):
        cum_c = jnp.dot(tri_incl, lw[ic * c:(ic + 1) * c, :], preferred_element_type=F32, precision=HIGHEST)
        cums.append(cum_c)
        cends.append(jnp.broadcast_to(cum_c[c - 1:c, :], (c, LANES)))
    cum = jnp.concatenate(cums, axis=0)
    cend = jnp.concatenate(cends, axis=0)
    einv = jnp.exp(-cum)
    eend = jnp.exp(cend - cum)
    a_til = kk * jnp.exp(cum - lw)
    r_til = r * jnp.exp(cum)
    k_til = kmod * einv
    b_til = bb * einv
    k_end = kmod * eend
    b_end = bb * eend

    chunks = range(nc)
    rows_of = lambda x, ic: x[ic * c:(ic + 1) * c]
    a_t = [bd(rows_of(a_til, ic)).astype(BF16) for ic in chunks]
    r_t = [bd(rows_of(r_til, ic)) for ic in chunks]
    k_t = [bd(rows_of(k_til, ic)).astype(BF16) for ic in chunks]
    b_t = [bd(rows_of(b_til, ic)).astype(BF16) for ic in chunks]
    k_e = [bd(rows_of(k_end, ic)).astype(BF16) for ic in chunks]
    b_e = [bd(rows_of(b_end, ic)).astype(BF16) for ic in chunks]
    v_b = [bd(rows_of(v, ic)).astype(BF16) for ic in chunks]
    w_end = [jnp.exp(cend[ic * c:ic * c + 1, :]) for ic in chunks]

    sc = [_dot_nt(jnp.concatenate([a_t[ic], r_t[ic].astype(BF16)], axis=0),
                  jnp.concatenate([k_t[ic], b_t[ic]], axis=0)) for ic in chunks]
    l_ab = [jnp.where(strict, s_[:LANES, LANES:], 0.0) for s_ in sc]
    a_ak = [jnp.where(strict, s_[:LANES, :LANES], 0.0).astype(BF16) for s_ in sc]
    a_rk = [jnp.where(incl, s_[LANES:, :LANES], 0.0).astype(BF16) for s_ in sc]
    a_rb = [jnp.where(incl, s_[LANES:, LANES:], 0.0).astype(BF16) for s_ in sc]

    x = [eye_f - jnp.where(level_masks[1], l_, 0.0) for l_ in l_ab]
    for m in (2, 4, 8, 16, 32):
        xc = [_inv_dot(x[ic], jnp.where(level_masks[m], l_ab[ic], 0.0)) for ic in chunks]
        x = [x[ic] - _inv_dot(xc[ic], x[ic]) for ic in chunks]

    akv = [_dot(a_ak[ic], v_b[ic]) for ic in chunks]
    rkv = [_dot(a_rk[ic], v_b[ic]) for ic in chunks]
    ktv = [_dot_tn(k_e[ic], v_b[ic]) for ic in chunks]
    ua = [_dot(x[ic].astype(BF16), jnp.concatenate([akv[ic].astype(BF16), a_t[ic]], axis=1)).astype(BF16)
          for ic in chunks]
    rb_ua = [_dot(a_rb[ic], ua[ic]) for ic in chunks]
    bt_ua = [_dot_tn(b_e[ic], ua[ic]) for ic in chunks]
    r_bars = [(r_t[ic] - rb_ua[ic][:, LANES:]).astype(BF16) for ic in chunks]
    y0 = [rkv[ic] - rb_ua[ic][:, :LANES] for ic in chunks]
    y0s = [y_[:c, :] + y_[c:, :] for y_ in y0]
    g_mats = [(jnp.where(eye, w_end[ic], 0.0) - bt_ua[ic][:, LANES:]).astype(BF16) for ic in chunks]
    h_mats = [ktv[ic] - bt_ua[ic][:, :LANES] for ic in chunks]

    p = p_ref[...]
    ys = []
    for ic in range(nc):
        pb = p.astype(BF16)
        y_bd = _dot(r_bars[ic], pb)
        ys.append(y_bd[:c, :] + y_bd[c:, :] + y0s[ic])
        p = _dot(g_mats[ic], pb) + h_mats[ic]
    p_ref[...] = p

    yv = jnp.concatenate(ys, axis=0)
    inv_d = 1.0 / RW_HEAD_DIM
    mean = head_sum(yv) * inv_d
    yc = yv - mean
    var = head_sum(yc * yc) * inv_d
    yn = yc * lax.rsqrt(var + RW_GN_EPS) * gng_ref[...] + gnb_ref[...]
    bonus = head_sum(r * kmod * rk_ref[...]) * v
    o_ref[0] = ((yn + bonus) * gate).astype(o_ref.dtype)


def _wkv(z_rw, mu, w0, w2, a0, a2, g2, k_k, k_a, r_k, gn_g, gn_b, ts):
    b, s, _ = z_rw.shape
    npair = RW_WIDTH // LANES
    lo_blk = (3 * RW_WIDTH) // (2 * LANES)
    row = lambda a: a.reshape(1, -1)

    def col(off):
        return pl.BlockSpec((1, ts, LANES), lambda ib, ip, it: (ib, it, off + ip))

    def vec(off):
        return pl.BlockSpec((1, LANES), lambda ib, ip, it: (0, off + ip))

    in_specs = [
        col(0), col(npair), col(2 * npair),
        pl.BlockSpec((1, ts, 2 * LANES), lambda ib, ip, it: (ib, it, lo_blk)),
        vec(0), vec(npair), vec(2 * npair),
        pl.BlockSpec((1, 2 * LANES), lambda ib, ip, it: (0, lo_blk)),
        vec(0), vec(0), vec(0), vec(0), vec(0), vec(0), vec(0),
        pl.BlockSpec((RW_DECAY_LORA, LANES), lambda ib, ip, it: (0, ip)),
        pl.BlockSpec((RW_ICLR_LORA, LANES), lambda ib, ip, it: (0, ip)),
        pl.BlockSpec((RW_GATE_LORA, LANES), lambda ib, ip, it: (0, ip)),
    ]
    scratch = [
        pltpu.VMEM((LANES, LANES), F32),
        pltpu.VMEM((1, LANES), F32), pltpu.VMEM((1, LANES), F32), pltpu.VMEM((1, LANES), F32),
        pltpu.VMEM((1, 2 * LANES), F32),
    ]
    mu2 = row(mu)
    return pl.pallas_call(
        _wkv_kernel,
        grid=(b, npair, s // ts),
        in_specs=in_specs,
        out_specs=pl.BlockSpec((1, ts, LANES), lambda ib, ip, it: (ib, it, ip)),
        out_shape=jax.ShapeDtypeStruct((b, s, RW_WIDTH), BF16),
        scratch_shapes=scratch,
        compiler_params=_cparams(("parallel", "parallel", "arbitrary")),
        name="wkv",
    )(z_rw, z_rw, z_rw, z_rw, mu2, mu2, mu2, mu2,
      row(w0), row(a0), row(k_k), row(k_a), row(r_k), row(gn_g), row(gn_b),
      w2.astype(BF16), a2.astype(BF16), g2.astype(BF16))


ATTN_ROWS = 16


def _attn_kernel(lq1_ref, lk1_ref, lq2_ref, lk2_ref, q_ref, k_ref, v_ref, subg_ref, o_ref,
                 qs_ref, s_ref, p_ref, m_ref, alpha_ref, acc_ref, *, lambda_init, tk):
    tq = q_ref.shape[1]
    rows = 2 * tq
    qi = pl.program_id(2)
    ngroups = rows // ATTN_ROWS
    ntiles = tk // LANES
    heads = (0, 1)

    lane_h0 = lax.broadcasted_iota(jnp.int32, (tq, LANES), 1) < DA_QK_DIM
    for h in heads:
        q = q_ref[0, :, h * LANES:(h + 1) * LANES]
        zero = jnp.zeros_like(q)
        qsc = q * (DA_QK_DIM ** -0.5)
        qs_ref[h] = jnp.concatenate([jnp.where(lane_h0, qsc, zero), jnp.where(lane_h0, zero, qsc)], axis=0)
    m_ref[...] = jnp.full_like(m_ref, -jnp.inf)
    alpha_ref[...] = jnp.zeros_like(alpha_ref)
    acc_ref[...] = jnp.zeros_like(acc_ref)
    p_ref[1] = jnp.zeros((rows, tk), BF16)

    key_chunk = lax.broadcasted_iota(jnp.int32, (ATTN_ROWS, LANES), 1) // CHUNK
    ones = jnp.ones((tk, LANES), BF16)

    def scores(h, j):
        koff = pl.multiple_of(j * tk, tk)
        s_ref[h] = _dot_nt(qs_ref[h], k_ref[0, pl.ds(koff, tk), h * LANES:(h + 1) * LANES])

    def accumulate(h, j):
        koff = pl.multiple_of(j * tk, tk)
        v_ext = jnp.concatenate([v_ref[0, pl.ds(koff, tk), h * LANES:(h + 1) * LANES], ones], axis=1)
        alpha = alpha_ref[h]
        acc_ref[h] = jnp.concatenate([alpha, alpha], axis=1) * acc_ref[h] + _dot(p_ref[h], v_ext)

    def softmax(h, j, masked):
        def load_scores(g, t):
            r0 = g * ATTN_ROWS
            s = s_ref[h, r0:r0 + ATTN_ROWS, t * LANES:(t + 1) * LANES]
            if masked:
                q_chunk = (qi * tq + r0 % tq) // CHUNK - j * (tk // CHUNK) - t * (LANES // CHUNK)
                s = jnp.where(key_chunk <= q_chunk, s, -jnp.inf)
            return s

        for g in range(ngroups):
            sl = slice(g * ATTN_ROWS, (g + 1) * ATTN_ROWS)
            m_prev = m_ref[h, sl, :]
            tile_max = load_scores(g, 0)
            for t in range(1, ntiles):
                tile_max = jnp.maximum(tile_max, load_scores(g, t))
            m_new = jnp.maximum(m_prev, jnp.max(tile_max, axis=-1, keepdims=True))
            alpha_ref[h, sl, :] = jnp.exp(m_prev - m_new)
            m_ref[h, sl, :] = m_new
        for g in range(ngroups):
            sl = slice(g * ATTN_ROWS, (g + 1) * ATTN_ROWS)
            m_row = m_ref[h, sl, :]
            for t in range(ntiles):
                p_ref[h, sl, t * LANES:(t + 1) * LANES] = jnp.exp(load_scores(g, t) - m_row).astype(BF16)

    def step(j, masked):
        softmax(0, j, masked)
        accumulate(1, jnp.maximum(j - 1, 0))
        scores(1, j)
        softmax(1, j, masked)
        accumulate(0, j)
        if not masked:
            scores(0, j + 1)

    nfull = (qi * tq + CHUNK) // tk
    scores(0, 0)

    def full_step(j, carry):
        step(j, False)
        return carry

    lax.fori_loop(0, nfull, full_step, 0)
    step(nfull, True)
    accumulate(1, nfull)

    lam = (jnp.exp(jnp.sum(lq1_ref[...] * lk1_ref[...], axis=-1, keepdims=True))
           - jnp.exp(jnp.sum(lq2_ref[...] * lk2_ref[...], axis=-1, keepdims=True))
           + lambda_init)
    for h in heads:
        acc = acc_ref[h]
        o = acc[:, :DA_V_DIM] / acc[:, DA_V_DIM:]
        o = o[:tq, :] - lam * o[tq:, :]
        o = _rms(o, subg_ref[...], DA_SUBLN_EPS) * (1.0 - lambda_init)
        o_ref[0, :, h * LANES:(h + 1) * LANES] = o.astype(o_ref.dtype)


def _attn(z_da, lq1, lk1, lq2, lk2, sub_g, lambda_init, tq, tk):
    b, s, _ = z_da.shape
    assert tk % tq == 0 and s % tk == 0
    row = lambda a: a.reshape(1, -1)
    lam_spec = pl.BlockSpec((1, DA_QK_DIM), lambda ib, ih, iq: (0, 0))
    kern = functools.partial(_attn_kernel, lambda_init=lambda_init, tk=tk)
    rows = 2 * tq
    pair = 2 * LANES
    npairs = DA_HEADS // 2
    rep = pltpu.VMEM((2, rows, LANES), F32)
    return pl.pallas_call(
        kern,
        grid=(b, npairs, s // tq),
        in_specs=[
            lam_spec, lam_spec, lam_spec, lam_spec,
            pl.BlockSpec((1, tq, pair), lambda ib, ih, iq: (ib, iq, ih)),
            pl.BlockSpec((1, s, pair), lambda ib, ih, iq: (ib, 0, npairs + ih)),
            pl.BlockSpec((1, s, pair), lambda ib, ih, iq: (ib, 0, 2 * npairs + ih)),
            pl.BlockSpec((1, DA_V_DIM), lambda ib, ih, iq: (0, 0)),
        ],
        out_specs=pl.BlockSpec((1, tq, pair), lambda ib, ih, iq: (ib, iq, ih)),
        out_shape=jax.ShapeDtypeStruct((b, s, DA_V_WIDTH), BF16),
        scratch_shapes=[
            pltpu.VMEM((2, rows, LANES), BF16),
            pltpu.VMEM((2, rows, tk), F32),
            pltpu.VMEM((2, rows, tk), BF16),
            rep, rep,
            pltpu.VMEM((2, rows, 2 * DA_V_DIM), F32),
        ],
        compiler_params=_cparams(("parallel", "parallel", "arbitrary")),
        name="attn",
    )(row(lq1), row(lk1), row(lq2), row(lk2), z_da, z_da, z_da, row(sub_g))


def _mix_kernel(x_ref, yrw_ref, yda_ref, grw_ref, gda_ref, wrw_ref, wda_ref, wo_ref, g_ref, o_ref):
    y_rw = _dot(yrw_ref[...], wrw_ref[...])
    y_da = _dot(yda_ref[...], wda_ref[...])
    merged = _sigmoid(grw_ref[...]) * y_rw + _sigmoid(gda_ref[...]) * y_da
    mo = _dot(merged.astype(BF16), wo_ref[...])
    o_ref[...] = x_ref[...] + _rms(mo, g_ref[...], RMS_EPS)


def _mix(x2d, y_rw, y_da, z_g, w_rw, w_da, w_o, g, tm):
    t, d = x2d.shape
    full = lambda shape: pl.BlockSpec(shape, lambda i: (0, 0))
    return pl.pallas_call(
        _mix_kernel,
        grid=(t // tm,),
        in_specs=[
            pl.BlockSpec((tm, d), lambda i: (i, 0)),
            pl.BlockSpec((tm, RW_WIDTH), lambda i: (i, 0)),
            pl.BlockSpec((tm, DA_V_WIDTH), lambda i: (i, 0)),
            pl.BlockSpec((tm, d), lambda i: (i, 0)),
            pl.BlockSpec((tm, d), lambda i: (i, 1)),
            full(w_rw.shape), full(w_da.shape), full(w_o.shape), full((1, d)),
        ],
        out_specs=pl.BlockSpec((tm, d), lambda i: (i, 0)),
        out_shape=jax.ShapeDtypeStruct((t, d), F32),
        compiler_params=_cparams(("parallel",)),
        name="mix",
    )(x2d, y_rw, y_da, z_g, z_g, w_rw, w_da, w_o, g)


def _mlp_kernel(x_ref, gpre_ref, wup_ref, wdn_ref, gpost_ref, o_ref, h_ref, acc_ref):
    j = pl.program_id(1)

    @pl.when(j == 0)
    def _():
        h_ref[...] = _rms(x_ref[...], gpre_ref[...], RMS_EPS).astype(BF16)
        acc_ref[...] = jnp.zeros_like(acc_ref)

    up = _dot(h_ref[...], wup_ref[...])
    act = jnp.square(jnp.maximum(up, 0.0))
    acc_ref[...] += _dot(act.astype(BF16), wdn_ref[...])

    @pl.when(j == pl.num_programs(1) - 1)
    def _():
        o_ref[...] = x_ref[...] + _rms(acc_ref[...], gpost_ref[...], RMS_EPS)


def _mlp(x2d, g_pre, w_up, w_dn, g_post, tm, tf):
    t, d = x2d.shape
    ff = w_up.shape[1]
    return pl.pallas_call(
        _mlp_kernel,
        grid=(t // tm, ff // tf),
        in_specs=[
            pl.BlockSpec((tm, d), lambda i, j: (i, 0)),
            pl.BlockSpec((1, d), lambda i, j: (0, 0)),
            pl.BlockSpec((d, tf), lambda i, j: (0, j)),
            pl.BlockSpec((tf, d), lambda i, j: (j, 0)),
            pl.BlockSpec((1, d), lambda i, j: (0, 0)),
        ],
        out_specs=pl.BlockSpec((tm, d), lambda i, j: (i, 0)),
        out_shape=jax.ShapeDtypeStruct((t, d), F32),
        scratch_shapes=[pltpu.VMEM((tm, d), BF16), pltpu.VMEM((tm, d), F32)],
        compiler_params=_cparams(("parallel", "arbitrary")),
        name="mlp",
    )(x2d, g_pre, w_up, w_dn, g_post)


def _tile(n, want):
    t = min(n, want)
    while n % t:
        t //= 2
    return t


def kernel(x, pre_mix_g, w_in, rw_shift_mu, rw_w0, rw_w2, rw_a0, rw_a2, rw_g2, rw_k_k, rw_k_a, rw_r_k, rw_gn_g, rw_gn_b, w_rw_out, da_lam_q1, da_lam_k1, da_lam_q2, da_lam_k2, da_sub_g, w_da_out, w_o, post_mix_g, pre_mlp_g, w_up, w_down, post_mlp_g):
    b, s, d = x.shape
    depth = w_in.shape[0]
    t = b * s
    tm = _tile(t, 512)
    for l in range(depth):
        lambda_init = 0.8 - 0.6 * math.exp(-0.3 * l)
        x2d = x.reshape(t, d)
        g_pre = pre_mix_g[l].reshape(1, d)
        z_rw, z_da, z_g = _inproj(x2d, g_pre, w_in[l].astype(BF16), _tile(t, 256))

        y_rw = _wkv(z_rw.reshape(b, s, RW_IN), rw_shift_mu[l], rw_w0[l], rw_w2[l], rw_a0[l],
                    rw_a2[l], rw_g2[l], rw_k_k[l], rw_k_a[l], rw_r_k[l], rw_gn_g[l], rw_gn_b[l],
                    _tile(s, 1024))
        y_da = _attn(z_da.reshape(b, s, DA_IN), da_lam_q1[l], da_lam_k1[l], da_lam_q2[l],
                     da_lam_k2[l], da_sub_g[l], lambda_init, _tile(s, 512), _tile(s, 1024))

        x1 = _mix(x2d, y_rw.reshape(t, RW_WIDTH), y_da.reshape(t, DA_V_WIDTH), z_g,
                  w_rw_out[l].astype(BF16), w_da_out[l].astype(BF16), w_o[l].astype(BF16),
                  post_mix_g[l].reshape(1, d), tm)
        x2 = _mlp(x1, pre_mlp_g[l].reshape(1, d), w_up[l].astype(BF16), w_down[l].astype(BF16),
                  post_mlp_g[l].reshape(1, d), _tile(t, 1024), 1024)
        x = x2.reshape(b, s, d)
    return x
```

```python
import functools
import math

import jax
import jax.numpy as jnp
from jax import lax
from jax.experimental import pallas as pl
from jax.experimental.pallas import tpu as pltpu

F32 = jnp.float32
BF16 = jnp.bfloat16

RMS_EPS = 1e-6
CHUNK = 64
RW_HEADS = 8
RW_HEAD_DIM = 64
RW_WIDTH = RW_HEADS * RW_HEAD_DIM
RW_DECAY_LORA = 64
RW_ICLR_LORA = 64
RW_GATE_LORA = 128
RW_GN_EPS = 1e-5 * RW_HEAD_DIM
RW_IN = 3 * RW_WIDTH + RW_DECAY_LORA + RW_ICLR_LORA + RW_GATE_LORA
DA_HEADS = 4
DA_QK_DIM = 64
DA_V_DIM = 2 * DA_QK_DIM
DA_QK_WIDTH = DA_HEADS * 2 * DA_QK_DIM
DA_V_WIDTH = DA_HEADS * DA_V_DIM
DA_IN = 2 * DA_QK_WIDTH + DA_V_WIDTH
DA_SUBLN_EPS = 1e-5

LANES = 128
WKV_CHUNK = 64
VMEM_LIMIT = 56 * 1024 * 1024


def _cparams(semantics):
    return pltpu.CompilerParams(dimension_semantics=semantics, vmem_limit_bytes=VMEM_LIMIT)


def _rms(x, g, eps):
    return x * lax.rsqrt(jnp.mean(x * x, axis=-1, keepdims=True) + eps) * g


def _dot(a, b):
    return jnp.dot(a, b, preferred_element_type=F32)


def _dot_nt(a, b):
    return lax.dot_general(a, b, (((1,), (1,)), ((), ())), preferred_element_type=F32)


def _dot_tn(a, b):
    return lax.dot_general(a, b, (((0,), (0,)), ((), ())), preferred_element_type=F32)


def _inproj_kernel(x_ref, g_ref, w_ref, orw_ref, oda_ref, og_ref):
    h = _rms(x_ref[...], g_ref[...], RMS_EPS).astype(BF16)
    o1, o2 = RW_IN, RW_IN + DA_IN
    orw_ref[...] = _dot(h, w_ref[:, :o1])
    oda_ref[...] = _dot(h, w_ref[:, o1:o2]).astype(oda_ref.dtype)
    og_ref[...] = _dot(h, w_ref[:, o2:])


def _inproj(x2d, g, w_bf16, tm):
    t, d = x2d.shape
    n = w_bf16.shape[1]
    n_gate = n - RW_IN - DA_IN
    return pl.pallas_call(
        _inproj_kernel,
        grid=(t // tm,),
        in_specs=[
            pl.BlockSpec((tm, d), lambda i: (i, 0)),
            pl.BlockSpec((1, d), lambda i: (0, 0)),
            pl.BlockSpec((d, n), lambda i: (0, 0)),
        ],
        out_specs=[
            pl.BlockSpec((tm, RW_IN), lambda i: (i, 0)),
            pl.BlockSpec((tm, DA_IN), lambda i: (i, 0)),
            pl.BlockSpec((tm, n_gate), lambda i: (i, 0)),
        ],
        out_shape=[
            jax.ShapeDtypeStruct((t, RW_IN), F32),
            jax.ShapeDtypeStruct((t, DA_IN), BF16),
            jax.ShapeDtypeStruct((t, n_gate), F32),
        ],
        compiler_params=_cparams(("parallel",)),
        name="inproj",
    )(x2d, g, w_bf16)


def _sigmoid(x):
    return 1.0 / (1.0 + jnp.exp(-x))


def _split_bf16(x, n):
    parts = []
    for _ in range(n - 1):
        hi = x.astype(BF16)
        parts.append(hi)
        x = x - hi.astype(F32)
    parts.append(x.astype(BF16))
    return parts


def _inv_dot(a, b):
    return _dot(a.astype(BF16), b.astype(BF16))


def _wkv_kernel(r_ref, k_ref, v_ref, lo_ref, mur_ref, muk_ref, muv_ref, mulo_ref,
                w0_ref, a0_ref, kk_ref, ka_ref, rk_ref, gng_ref, gnb_ref,
                w2_ref, a2_ref, g2_ref, o_ref,
                p_ref, cr_ref, ck_ref, cv_ref, clo_ref):
    ts = r_ref.shape[1]
    nc = ts // WKV_CHUNK
    c = WKV_CHUNK

    @pl.when(pl.program_id(2) == 0)
    def _():
        p_ref[...] = jnp.zeros_like(p_ref)
        cr_ref[...] = jnp.zeros_like(cr_ref)
        ck_ref[...] = jnp.zeros_like(ck_ref)
        cv_ref[...] = jnp.zeros_like(cv_ref)
        clo_ref[...] = jnp.zeros_like(clo_ref)

    def shifted(z, carry_ref, mu):
        prev = pltpu.roll(z, 1, axis=0)
        row = lax.broadcasted_iota(jnp.int32, z.shape, 0)
        prev = jnp.where(row == 0, carry_ref[...], prev)
        carry_ref[...] = z[ts - 1:ts, :]
        return z + (prev - z) * mu

    r = shifted(r_ref[0], cr_ref, mur_ref[...])
    k = shifted(k_ref[0], ck_ref, muk_ref[...])
    v = shifted(v_ref[0], cv_ref, muv_ref[...])
    lo = shifted(lo_ref[0], clo_ref, mulo_ref[...])
    xwa = lo[:, :LANES]
    xg = lo[:, LANES:]

    zeros64 = jnp.zeros((RW_DECAY_LORA, LANES), BF16)
    w2p = jnp.concatenate([w2_ref[...], zeros64], axis=0)
    a2p = jnp.concatenate([zeros64, a2_ref[...]], axis=0)
    wl = _dot(jnp.tanh(xwa).astype(BF16), w2p)
    al = _dot(xwa.astype(BF16), a2p)
    gate = _dot(_sigmoid(xg).astype(BF16), g2_ref[...])

    y = -(w0_ref[...] + wl)
    softplus = jnp.maximum(y, 0.0) + jnp.log(1.0 + jnp.exp(-jnp.abs(y)))
    lw = -jnp.exp(-softplus - 0.5)
    a = _sigmoid(a0_ref[...] + al)

    lane = lax.broadcasted_iota(jnp.int32, (LANES, LANES), 1)
    rowi = lax.broadcasted_iota(jnp.int32, (LANES, LANES), 0)
    same_head = (lane // RW_HEAD_DIM) == (rowi // RW_HEAD_DIM)
    head_ones = jnp.where(same_head, 1.0, 0.0).astype(BF16)

    def head_sum(x):
        hi, lo = _split_bf16(x, 2)
        return _dot(hi, head_ones) + _dot(lo, head_ones)

    kk = k * kk_ref[...]
    kk = kk / jnp.maximum(jnp.sqrt(head_sum(kk * kk)), 1e-12)
    kmod = k * (1.0 + (a - 1.0) * ka_ref[...])

    bb = kk * a

    ri = lax.broadcasted_iota(jnp.int32, (c, c), 0)
    ci = lax.broadcasted_iota(jnp.int32, (c, c), 1)
    tri_incl = jnp.where(ci <= ri, 1.0, 0.0).astype(BF16)
    lane_h0 = lax.broadcasted_iota(jnp.int32, (c, LANES), 1) < RW_HEAD_DIM
    strict = lane < rowi
    incl = lane <= rowi
    eye = lane == rowi
    eye_f = jnp.where(eye, 1.0, 0.0)

    def bd(x):
        return jnp.concatenate([jnp.where(lane_h0, x, 0.0), jnp.where(lane_h0, 0.0, x)], axis=0)

    def level_mask(m):
        return ((rowi // (2 * m)) == (lane // (2 * m))) & ((rowi % (2 * m)) >= m) & ((lane % (2 * m)) < m)

    level_masks = {m: level_mask(m) for m in (1, 2, 4, 8, 16, 32)}

    lw_parts = jnp.concatenate(_split_bf16(lw, 3), axis=1)
    cums, cends = [], []
    for ic in range(nc):
        cum3 = _dot(tri_incl, lw_parts[ic * c:(ic + 1) * c, :])
        cum_c = cum3[:, :LANES] + cum3[:, LANES:2 * LANES] + cum3[:, 2 * LANES:]
        cums.append(cum_c)
        cends.append(jnp.broadcast_to(cum_c[c - 1:c, :], (c, LANES)))
    cum = jnp.concatenate(cums, axis=0)
    cend = jnp.concatenate(cends, axis=0)
    einv = jnp.exp(-cum)
    eend = jnp.exp(cend - cum)
    a_til = kk * jnp.exp(cum - lw)
    r_til = r * jnp.exp(cum)
    k_til = kmod * einv
    b_til = bb * einv
    k_end = kmod * eend
    b_end = bb * eend

    chunks = range(nc)
    rows_of = lambda x, ic: x[ic * c:(ic + 1) * c]
    a_t = [bd(rows_of(a_til, ic)).astype(BF16) for ic in chunks]
    r_t = [bd(rows_of(r_til, ic)) for ic in chunks]
    k_t = [bd(rows_of(k_til, ic)).astype(BF16) for ic in chunks]
    b_t = [bd(rows_of(b_til, ic)).astype(BF16) for ic in chunks]
    k_e = [bd(rows_of(k_end, ic)).astype(BF16) for ic in chunks]
    b_e = [bd(rows_of(b_end, ic)).astype(BF16) for ic in chunks]
    v_b = [bd(rows_of(v, ic)).astype(BF16) for ic in chunks]
    w_end = [jnp.exp(cend[ic * c:ic * c + 1, :]) for ic in chunks]

    sc = [_dot_nt(jnp.concatenate([a_t[ic], r_t[ic].astype(BF16)], axis=0),
                  jnp.concatenate([k_t[ic], b_t[ic]], axis=0)) for ic in chunks]
    l_ab = [jnp.where(strict, s_[:LANES, LANES:], 0.0) for s_ in sc]
    a_ak = [jnp.where(strict, s_[:LANES, :LANES], 0.0).astype(BF16) for s_ in sc]
    a_rk = [jnp.where(incl, s_[LANES:, :LANES], 0.0).astype(BF16) for s_ in sc]
    a_rb = [jnp.where(incl, s_[LANES:, LANES:], 0.0).astype(BF16) for s_ in sc]

    x = [eye_f - jnp.where(level_masks[1], l_, 0.0) for l_ in l_ab]
    for m in (2, 4, 8, 16, 32):
        xc = [_inv_dot(x[ic], jnp.where(level_masks[m], l_ab[ic], 0.0)) for ic in chunks]
        x = [x[ic] - _inv_dot(xc[ic], x[ic]) for ic in chunks]

    akv = [_dot(a_ak[ic], v_b[ic]) for ic in chunks]
    rkv = [_dot(a_rk[ic], v_b[ic]) for ic in chunks]
    ktv = [_dot_tn(k_e[ic], v_b[ic]) for ic in chunks]
    ua = [_dot(x[ic].astype(BF16), jnp.concatenate([akv[ic].astype(BF16), a_t[ic]], axis=1)).astype(BF16)
          for ic in chunks]
    rb_ua = [_dot(a_rb[ic], ua[ic]) for ic in chunks]
    bt_ua = [_dot_tn(b_e[ic], ua[ic]) for ic in chunks]
    r_bars = [(r_t[ic] - rb_ua[ic][:, LANES:]).astype(BF16) for ic in chunks]
    y0 = [rkv[ic] - rb_ua[ic][:, :LANES] for ic in chunks]
    y0s = [y_[:c, :] + y_[c:, :] for y_ in y0]
    g_mats = [(jnp.where(eye, w_end[ic], 0.0) - bt_ua[ic][:, LANES:]).astype(BF16) for ic in chunks]
    h_mats = [ktv[ic] - bt_ua[ic][:, :LANES] for ic in chunks]

    p = p_ref[...]
    ys = []
    for ic in range(nc):
        pb = p.astype(BF16)
        y_bd = _dot(r_bars[ic], pb)
        ys.append(y_bd[:c, :] + y_bd[c:, :] + y0s[ic])
        p = _dot(g_mats[ic], pb) + h_mats[ic]
    p_ref[...] = p

    yv = jnp.concatenate(ys, axis=0)
    inv_d = 1.0 / RW_HEAD_DIM
    mean = head_sum(yv) * inv_d
    yc = yv - mean
    var = head_sum(yc * yc) * inv_d
    yn = yc * lax.rsqrt(var + RW_GN_EPS) * gng_ref[...] + gnb_ref[...]
    bonus = head_sum(r * kmod * rk_ref[...]) * v
    o_ref[0] = ((yn + bonus) * gate).astype(o_ref.dtype)


def _wkv(z_rw, mu, w0, w2, a0, a2, g2, k_k, k_a, r_k, gn_g, gn_b, ts):
    b, s, _ = z_rw.shape
    npair = RW_WIDTH // LANES
    lo_blk = (3 * RW_WIDTH) // (2 * LANES)
    row = lambda a: a.reshape(1, -1)

    def col(off):
        return pl.BlockSpec((1, ts, LANES), lambda ib, ip, it: (ib, it, off + ip))

    def vec(off):
        return pl.BlockSpec((1, LANES), lambda ib, ip, it: (0, off + ip))

    in_specs = [
        col(0), col(npair), col(2 * npair),
        pl.BlockSpec((1, ts, 2 * LANES), lambda ib, ip, it: (ib, it, lo_blk)),
        vec(0), vec(npair), vec(2 * npair),
        pl.BlockSpec((1, 2 * LANES), lambda ib, ip, it: (0, lo_blk)),
        vec(0), vec(0), vec(0), vec(0), vec(0), vec(0), vec(0),
        pl.BlockSpec((RW_DECAY_LORA, LANES), lambda ib, ip, it: (0, ip)),
        pl.BlockSpec((RW_ICLR_LORA, LANES), lambda ib, ip, it: (0, ip)),
        pl.BlockSpec((RW_GATE_LORA, LANES), lambda ib, ip, it: (0, ip)),
    ]
    scratch = [
        pltpu.VMEM((LANES, LANES), F32),
        pltpu.VMEM((1, LANES), F32), pltpu.VMEM((1, LANES), F32), pltpu.VMEM((1, LANES), F32),
        pltpu.VMEM((1, 2 * LANES), F32),
    ]
    mu2 = row(mu)
    return pl.pallas_call(
        _wkv_kernel,
        grid=(b, npair, s // ts),
        in_specs=in_specs,
        out_specs=pl.BlockSpec((1, ts, LANES), lambda ib, ip, it: (ib, it, ip)),
        out_shape=jax.ShapeDtypeStruct((b, s, RW_WIDTH), BF16),
        scratch_shapes=scratch,
        compiler_params=_cparams(("parallel", "parallel", "arbitrary")),
        name="wkv",
    )(z_rw, z_rw, z_rw, z_rw, mu2, mu2, mu2, mu2,
      row(w0), row(a0), row(k_k), row(k_a), row(r_k), row(gn_g), row(gn_b),
      w2.astype(BF16), a2.astype(BF16), g2.astype(BF16))


ATTN_ROWS = 16


def _attn_kernel(lq1_ref, lk1_ref, lq2_ref, lk2_ref, q_ref, k_ref, v_ref, subg_ref, o_ref,
                 qs_ref, s_ref, p_ref, m_ref, alpha_ref, acc_ref, *, lambda_init, tk):
    tq = q_ref.shape[1]
    rows = 2 * tq
    qi = pl.program_id(2)
    ngroups = rows // ATTN_ROWS
    ntiles = tk // LANES
    heads = (0, 1)

    lane_h0 = lax.broadcasted_iota(jnp.int32, (tq, LANES), 1) < DA_QK_DIM
    for h in heads:
        q = q_ref[0, :, h * LANES:(h + 1) * LANES]
        zero = jnp.zeros_like(q)
        qsc = q * (DA_QK_DIM ** -0.5)
        qs_ref[h] = jnp.concatenate([jnp.where(lane_h0, qsc, zero), jnp.where(lane_h0, zero, qsc)], axis=0)
    m_ref[...] = jnp.full_like(m_ref, -jnp.inf)
    alpha_ref[...] = jnp.zeros_like(alpha_ref)
    acc_ref[...] = jnp.zeros_like(acc_ref)
    p_ref[1] = jnp.zeros((rows, tk), BF16)

    key_chunk = lax.broadcasted_iota(jnp.int32, (ATTN_ROWS, LANES), 1) // CHUNK
    ones = jnp.ones((tk, LANES), BF16)

    def scores(h, j):
        koff = pl.multiple_of(j * tk, tk)
        s_ref[h] = _dot_nt(qs_ref[h], k_ref[0, pl.ds(koff, tk), h * LANES:(h + 1) * LANES])

    def accumulate(h, j):
        koff = pl.multiple_of(j * tk, tk)
        v_ext = jnp.concatenate([v_ref[0, pl.ds(koff, tk), h * LANES:(h + 1) * LANES], ones], axis=1)
        alpha = alpha_ref[h]
        acc_ref[h] = jnp.concatenate([alpha, alpha], axis=1) * acc_ref[h] + _dot(p_ref[h], v_ext)

    def softmax(h, j, masked):
        def load_scores(g, t):
            r0 = g * ATTN_ROWS
            s = s_ref[h, r0:r0 + ATTN_ROWS, t * LANES:(t + 1) * LANES]
            if masked:
                q_chunk = (qi * tq + r0 % tq) // CHUNK - j * (tk // CHUNK) - t * (LANES // CHUNK)
                s = jnp.where(key_chunk <= q_chunk, s, -jnp.inf)
            return s

        for g in range(ngroups):
            sl = slice(g * ATTN_ROWS, (g + 1) * ATTN_ROWS)
            m_prev = m_ref[h, sl, :]
            tile_max = load_scores(g, 0)
            for t in range(1, ntiles):
                tile_max = jnp.maximum(tile_max, load_scores(g, t))
            m_new = jnp.maximum(m_prev, jnp.max(tile_max, axis=-1, keepdims=True))
            alpha_ref[h, sl, :] = jnp.exp(m_prev - m_new)
            m_ref[h, sl, :] = m_new
        for g in range(ngroups):
            sl = slice(g * ATTN_ROWS, (g + 1) * ATTN_ROWS)
            m_row = m_ref[h, sl, :]
            for t in range(ntiles):
                p_ref[h, sl, t * LANES:(t + 1) * LANES] = jnp.exp(load_scores(g, t) - m_row).astype(BF16)

    def step(j, masked):
        softmax(0, j, masked)
        accumulate(1, jnp.maximum(j - 1, 0))
        scores(1, j)
        softmax(1, j, masked)
        accumulate(0, j)
        if not masked:
            scores(0, j + 1)

    nfull = (qi * tq + CHUNK) // tk
    scores(0, 0)

    def full_step(j, carry):
        step(j, False)
        return carry

    lax.fori_loop(0, nfull, full_step, 0)
    step(nfull, True)
    accumulate(1, nfull)

    lam = (jnp.exp(jnp.sum(lq1_ref[...] * lk1_ref[...], axis=-1, keepdims=True))
           - jnp.exp(jnp.sum(lq2_ref[...] * lk2_ref[...], axis=-1, keepdims=True))
           + lambda_init)
    for h in heads:
        acc = acc_ref[h]
        o = acc[:, :DA_V_DIM] / acc[:, DA_V_DIM:]
        o = o[:tq, :] - lam * o[tq:, :]
        o = _rms(o, subg_ref[...], DA_SUBLN_EPS) * (1.0 - lambda_init)
        o_ref[0, :, h * LANES:(h + 1) * LANES] = o.astype(o_ref.dtype)


def _attn(z_da, lq1, lk1, lq2, lk2, sub_g, lambda_init, tq, tk):
    b, s, _ = z_da.shape
    assert tk % tq == 0 and s % tk == 0
    row = lambda a: a.reshape(1, -1)
    lam_spec = pl.BlockSpec((1, DA_QK_DIM), lambda ib, ih, iq: (0, 0))
    kern = functools.partial(_attn_kernel, lambda_init=lambda_init, tk=tk)
    rows = 2 * tq
    pair = 2 * LANES
    npairs = DA_HEADS // 2
    rep = pltpu.VMEM((2, rows, LANES), F32)
    return pl.pallas_call(
        kern,
        grid=(b, npairs, s // tq),
        in_specs=[
            lam_spec, lam_spec, lam_spec, lam_spec,
            pl.BlockSpec((1, tq, pair), lambda ib, ih, iq: (ib, iq, ih)),
            pl.BlockSpec((1, s, pair), lambda ib, ih, iq: (ib, 0, npairs + ih)),
            pl.BlockSpec((1, s, pair), lambda ib, ih, iq: (ib, 0, 2 * npairs + ih)),
            pl.BlockSpec((1, DA_V_DIM), lambda ib, ih, iq: (0, 0)),
        ],
        out_specs=pl.BlockSpec((1, tq, pair), lambda ib, ih, iq: (ib, iq, ih)),
        out_shape=jax.ShapeDtypeStruct((b, s, DA_V_WIDTH), BF16),
        scratch_shapes=[
            pltpu.VMEM((2, rows, LANES), BF16),
            pltpu.VMEM((2, rows, tk), F32),
            pltpu.VMEM((2, rows, tk), BF16),
            rep, rep,
            pltpu.VMEM((2, rows, 2 * DA_V_DIM), F32),
        ],
        compiler_params=_cparams(("parallel", "parallel", "arbitrary")),
        name="attn",
    )(row(lq1), row(lk1), row(lq2), row(lk2), z_da, z_da, z_da, row(sub_g))


def _mix_kernel(x_ref, yrw_ref, yda_ref, grw_ref, gda_ref, wrw_ref, wda_ref, wo_ref, g_ref, o_ref):
    y_rw = _dot(yrw_ref[...], wrw_ref[...])
    y_da = _dot(yda_ref[...], wda_ref[...])
    merged = _sigmoid(grw_ref[...]) * y_rw + _sigmoid(gda_ref[...]) * y_da
    mo = _dot(merged.astype(BF16), wo_ref[...])
    o_ref[...] = x_ref[...] + _rms(mo, g_ref[...], RMS_EPS)


def _mix(x2d, y_rw, y_da, z_g, w_rw, w_da, w_o, g, tm):
    t, d = x2d.shape
    full = lambda shape: pl.BlockSpec(shape, lambda i: (0, 0))
    return pl.pallas_call(
        _mix_kernel,
        grid=(t // tm,),
        in_specs=[
            pl.BlockSpec((tm, d), lambda i: (i, 0)),
            pl.BlockSpec((tm, RW_WIDTH), lambda i: (i, 0)),
            pl.BlockSpec((tm, DA_V_WIDTH), lambda i: (i, 0)),
            pl.BlockSpec((tm, d), lambda i: (i, 0)),
            pl.BlockSpec((tm, d), lambda i: (i, 1)),
            full(w_rw.shape), full(w_da.shape), full(w_o.shape), full((1, d)),
        ],
        out_specs=pl.BlockSpec((tm, d), lambda i: (i, 0)),
        out_shape=jax.ShapeDtypeStruct((t, d), F32),
        compiler_params=_cparams(("parallel",)),
        name="mix",
    )(x2d, y_rw, y_da, z_g, z_g, w_rw, w_da, w_o, g)


def _mlp_kernel(x_ref, gpre_ref, wup_ref, wdn_ref, gpost_ref, o_ref, h_ref, acc_ref):
    j = pl.program_id(1)

    @pl.when(j == 0)
    def _():
        h_ref[...] = _rms(x_ref[...], gpre_ref[...], RMS_EPS).astype(BF16)
        acc_ref[...] = jnp.zeros_like(acc_ref)

    up = _dot(h_ref[...], wup_ref[...])
    act = jnp.square(jnp.maximum(up, 0.0))
    acc_ref[...] += _dot(act.astype(BF16), wdn_ref[...])

    @pl.when(j == pl.num_programs(1) - 1)
    def _():
        o_ref[...] = x_ref[...] + _rms(acc_ref[...], gpost_ref[...], RMS_EPS)


def _mlp(x2d, g_pre, w_up, w_dn, g_post, tm, tf):
    t, d = x2d.shape
    ff = w_up.shape[1]
    return pl.pallas_call(
        _mlp_kernel,
        grid=(t // tm, ff // tf),
        in_specs=[
            pl.BlockSpec((tm, d), lambda i, j: (i, 0)),
            pl.BlockSpec((1, d), lambda i, j: (0, 0)),
            pl.BlockSpec((d, tf), lambda i, j: (0, j)),
            pl.BlockSpec((tf, d), lambda i, j: (j, 0)),
            pl.BlockSpec((1, d), lambda i, j: (0, 0)),
        ],
        out_specs=pl.BlockSpec((tm, d), lambda i, j: (i, 0)),
        out_shape=jax.ShapeDtypeStruct((t, d), F32),
        scratch_shapes=[pltpu.VMEM((tm, d), BF16), pltpu.VMEM((tm, d), F32)],
        compiler_params=_cparams(("parallel", "arbitrary")),
        name="mlp",
    )(x2d, g_pre, w_up, w_dn, g_post)


def _tile(n, want):
    t = min(n, want)
    while n % t:
        t //= 2
    return t


def kernel(x, pre_mix_g, w_in, rw_shift_mu, rw_w0, rw_w2, rw_a0, rw_a2, rw_g2, rw_k_k, rw_k_a, rw_r_k, rw_gn_g, rw_gn_b, w_rw_out, da_lam_q1, da_lam_k1, da_lam_q2, da_lam_k2, da_sub_g, w_da_out, w_o, post_mix_g, pre_mlp_g, w_up, w_down, post_mlp_g):
    b, s, d = x.shape
    depth = w_in.shape[0]
    t = b * s
    tm = _tile(t, 512)
    for l in range(depth):
        lambda_init = 0.8 - 0.6 * math.exp(-0.3 * l)
        x2d = x.reshape(t, d)
        g_pre = pre_mix_g[l].reshape(1, d)
        z_rw, z_da, z_g = _inproj(x2d, g_pre, w_in[l].astype(BF16), _tile(t, 256))

        y_rw = _wkv(z_rw.reshape(b, s, RW_IN), rw_shift_mu[l], rw_w0[l], rw_w2[l], rw_a0[l],
                    rw_a2[l], rw_g2[l], rw_k_k[l], rw_k_a[l], rw_r_k[l], rw_gn_g[l], rw_gn_b[l],
                    _tile(s, 1024))
        y_da = _attn(z_da.reshape(b, s, DA_IN), da_lam_q1[l], da_lam_k1[l], da_lam_q2[l],
                     da_lam_k2[l], da_sub_g[l], lambda_init, _tile(s, 512), _tile(s, 1024))

        x1 = _mix(x2d, y_rw.reshape(t, RW_WIDTH), y_da.reshape(t, DA_V_WIDTH), z_g,
                  w_rw_out[l].astype(BF16), w_da_out[l].astype(BF16), w_o[l].astype(BF16),
                  post_mix_g[l].reshape(1, d), tm)
        x2 = _mlp(x1, pre_mlp_g[l].reshape(1, d), w_up[l].astype(BF16), w_down[l].astype(BF16),
                  post_mlp_g[l].reshape(1, d), _tile(t, 1024), 1024)
        x = x2.reshape(b, s, d)
    return x
```

```python
import functools
import math

import jax
import jax.numpy as jnp
from jax import lax
from jax.experimental import pallas as pl
from jax.experimental.pallas import tpu as pltpu

F32 = jnp.float32
BF16 = jnp.bfloat16

RMS_EPS = 1e-6
CHUNK = 64
RW_HEADS = 8
RW_HEAD_DIM = 64
RW_WIDTH = RW_HEADS * RW_HEAD_DIM
RW_DECAY_LORA = 64
RW_ICLR_LORA = 64
RW_GATE_LORA = 128
RW_GN_EPS = 1e-5 * RW_HEAD_DIM
RW_IN = 3 * RW_WIDTH + RW_DECAY_LORA + RW_ICLR_LORA + RW_GATE_LORA
DA_HEADS = 4
DA_QK_DIM = 64
DA_V_DIM = 2 * DA_QK_DIM
DA_QK_WIDTH = DA_HEADS * 2 * DA_QK_DIM
DA_V_WIDTH = DA_HEADS * DA_V_DIM
DA_IN = 2 * DA_QK_WIDTH + DA_V_WIDTH
DA_SUBLN_EPS = 1e-5

LANES = 128
WKV_CHUNK = 64
VMEM_LIMIT = 56 * 1024 * 1024


def _cparams(semantics):
    return pltpu.CompilerParams(dimension_semantics=semantics, vmem_limit_bytes=VMEM_LIMIT)


def _rms(x, g, eps):
    return x * lax.rsqrt(jnp.mean(x * x, axis=-1, keepdims=True) + eps) * g


def _dot(a, b):
    return jnp.dot(a, b, preferred_element_type=F32)


def _dot_nt(a, b):
    return lax.dot_general(a, b, (((1,), (1,)), ((), ())), preferred_element_type=F32)


def _dot_tn(a, b):
    return lax.dot_general(a, b, (((0,), (0,)), ((), ())), preferred_element_type=F32)


def _inproj_kernel(x_ref, g_ref, w_ref, orw_ref, oda_ref, og_ref):
    h = _rms(x_ref[...], g_ref[...], RMS_EPS).astype(BF16)
    o1, o2 = RW_IN, RW_IN + DA_IN
    orw_ref[...] = _dot(h, w_ref[:, :o1])
    oda_ref[...] = _dot(h, w_ref[:, o1:o2]).astype(oda_ref.dtype)
    og_ref[...] = _dot(h, w_ref[:, o2:])


def _inproj(x2d, g, w_bf16, tm):
    t, d = x2d.shape
    n = w_bf16.shape[1]
    n_gate = n - RW_IN - DA_IN
    return pl.pallas_call(
        _inproj_kernel,
        grid=(t // tm,),
        in_specs=[
            pl.BlockSpec((tm, d), lambda i: (i, 0)),
            pl.BlockSpec((1, d), lambda i: (0, 0)),
            pl.BlockSpec((d, n), lambda i: (0, 0)),
        ],
        out_specs=[
            pl.BlockSpec((tm, RW_IN), lambda i: (i, 0)),
            pl.BlockSpec((tm, DA_IN), lambda i: (i, 0)),
            pl.BlockSpec((tm, n_gate), lambda i: (i, 0)),
        ],
        out_shape=[
            jax.ShapeDtypeStruct((t, RW_IN), F32),
            jax.ShapeDtypeStruct((t, DA_IN), BF16),
            jax.ShapeDtypeStruct((t, n_gate), F32),
        ],
        compiler_params=_cparams(("parallel",)),
        name="inproj",
    )(x2d, g, w_bf16)


def _sigmoid(x):
    return 1.0 / (1.0 + jnp.exp(-x))


def _split_bf16(x, n):
    parts = []
    for _ in range(n - 1):
        hi = x.astype(BF16)
        parts.append(hi)
        x = x - hi.astype(F32)
    parts.append(x.astype(BF16))
    return parts


def _inv_dot(a, b):
    return _dot(a.astype(BF16), b.astype(BF16))


def _wkv_kernel(r_ref, k_ref, v_ref, lo_ref, mur_ref, muk_ref, muv_ref, mulo_ref,
                w0_ref, a0_ref, kk_ref, ka_ref, rk_ref, gng_ref, gnb_ref,
                w2_ref, a2_ref, g2_ref, o_ref,
                p_ref, cr_ref, ck_ref, cv_ref, clo_ref):
    ts = r_ref.shape[1]
    nc = ts // WKV_CHUNK
    c = WKV_CHUNK

    @pl.when(pl.program_id(2) == 0)
    def _():
        p_ref[...] = jnp.zeros_like(p_ref)
        cr_ref[...] = jnp.zeros_like(cr_ref)
        ck_ref[...] = jnp.zeros_like(ck_ref)
        cv_ref[...] = jnp.zeros_like(cv_ref)
        clo_ref[...] = jnp.zeros_like(clo_ref)

    def shifted(z, carry_ref, mu):
        prev = pltpu.roll(z, 1, axis=0)
        row = lax.broadcasted_iota(jnp.int32, z.shape, 0)
        prev = jnp.where(row == 0, carry_ref[...], prev)
        carry_ref[...] = z[ts - 1:ts, :]
        return z + (prev - z) * mu

    r = shifted(r_ref[0], cr_ref, mur_ref[...])
    k = shifted(k_ref[0], ck_ref, muk_ref[...])
    v = shifted(v_ref[0], cv_ref, muv_ref[...])
    lo = shifted(lo_ref[0], clo_ref, mulo_ref[...])
    xwa = lo[:, :LANES]
    xg = lo[:, LANES:]

    zeros64 = jnp.zeros((RW_DECAY_LORA, LANES), BF16)
    w2p = jnp.concatenate([w2_ref[...], zeros64], axis=0)
    a2p = jnp.concatenate([zeros64, a2_ref[...]], axis=0)
    wl = _dot(jnp.tanh(xwa).astype(BF16), w2p)
    al = _dot(xwa.astype(BF16), a2p)
    gate = _dot(_sigmoid(xg).astype(BF16), g2_ref[...])

    y = -(w0_ref[...] + wl)
    softplus = jnp.maximum(y, 0.0) + jnp.log(1.0 + jnp.exp(-jnp.abs(y)))
    lw = -jnp.exp(-softplus - 0.5)
    a = _sigmoid(a0_ref[...] + al)

    lane = lax.broadcasted_iota(jnp.int32, (LANES, LANES), 1)
    rowi = lax.broadcasted_iota(jnp.int32, (LANES, LANES), 0)
    same_head = (lane // RW_HEAD_DIM) == (rowi // RW_HEAD_DIM)
    head_ones = jnp.where(same_head, 1.0, 0.0).astype(BF16)

    def head_sum(x):
        hi, lo = _split_bf16(x, 2)
        return _dot(hi, head_ones) + _dot(lo, head_ones)

    kk = k * kk_ref[...]
    kk = kk / jnp.maximum(jnp.sqrt(head_sum(kk * kk)), 1e-12)
    kmod = k * (1.0 + (a - 1.0) * ka_ref[...])

    bb = kk * a

    ri = lax.broadcasted_iota(jnp.int32, (c, c), 0)
    ci = lax.broadcasted_iota(jnp.int32, (c, c), 1)
    tri_incl = jnp.where(ci <= ri, 1.0, 0.0).astype(BF16)
    lane_h0 = lax.broadcasted_iota(jnp.int32, (c, LANES), 1) < RW_HEAD_DIM
    strict = lane < rowi
    incl = lane <= rowi
    eye = lane == rowi
    eye_f = jnp.where(eye, 1.0, 0.0)

    def bd(x):
        return jnp.concatenate([jnp.where(lane_h0, x, 0.0), jnp.where(lane_h0, 0.0, x)], axis=0)

    def level_mask(m):
        return ((rowi // (2 * m)) == (lane // (2 * m))) & ((rowi % (2 * m)) >= m) & ((lane % (2 * m)) < m)

    level_masks = {m: level_mask(m) for m in (1, 2, 4, 8, 16, 32)}

    lw_parts = jnp.concatenate(_split_bf16(lw, 3), axis=1)
    cums, cends = [], []
    for ic in range(nc):
        cum3 = _dot(tri_incl, lw_parts[ic * c:(ic + 1) * c, :])
        cum_c = cum3[:, :LANES] + cum3[:, LANES:2 * LANES] + cum3[:, 2 * LANES:]
        cums.append(cum_c)
        cends.append(jnp.broadcast_to(cum_c[c - 1:c, :], (c, LANES)))
    cum = jnp.concatenate(cums, axis=0)
    cend = jnp.concatenate(cends, axis=0)
    einv = jnp.exp(-cum)
    eend = jnp.exp(cend - cum)
    a_til = kk * jnp.exp(cum - lw)
    r_til = r * jnp.exp(cum)
    k_til = kmod * einv
    b_til = bb * einv
    k_end = kmod * eend
    b_end = bb * eend

    chunks = range(nc)
    rows_of = lambda x, ic: x[ic * c:(ic + 1) * c]
    a_t = [bd(rows_of(a_til, ic)).astype(BF16) for ic in chunks]
    r_t = [bd(rows_of(r_til, ic)) for ic in chunks]
    k_t = [bd(rows_of(k_til, ic)).astype(BF16) for ic in chunks]
    b_t = [bd(rows_of(b_til, ic)).astype(BF16) for ic in chunks]
    k_e = [bd(rows_of(k_end, ic)).astype(BF16) for ic in chunks]
    b_e = [bd(rows_of(b_end, ic)).astype(BF16) for ic in chunks]
    v_b = [bd(rows_of(v, ic)).astype(BF16) for ic in chunks]
    w_end = [jnp.exp(cend[ic * c:ic * c + 1, :]) for ic in chunks]

    sc = [_dot_nt(jnp.concatenate([a_t[ic], r_t[ic].astype(BF16)], axis=0),
                  jnp.concatenate([k_t[ic], b_t[ic]], axis=0)) for ic in chunks]
    l_ab = [jnp.where(strict, s_[:LANES, LANES:], 0.0) for s_ in sc]
    a_ak = [jnp.where(strict, s_[:LANES, :LANES], 0.0).astype(BF16) for s_ in sc]
    a_rk = [jnp.where(incl, s_[LANES:, :LANES], 0.0).astype(BF16) for s_ in sc]
    a_rb = [jnp.where(incl, s_[LANES:, LANES:], 0.0).astype(BF16) for s_ in sc]

    x = [eye_f - jnp.where(level_masks[1], l_, 0.0) for l_ in l_ab]
    for m in (2, 4, 8, 16, 32):
        xc = [_inv_dot(x[ic], jnp.where(level_masks[m], l_ab[ic], 0.0)) for ic in chunks]
        x = [x[ic] - _inv_dot(xc[ic], x[ic]) for ic in chunks]

    akv = [_dot(a_ak[ic], v_b[ic]) for ic in chunks]
    rkv = [_dot(a_rk[ic], v_b[ic]) for ic in chunks]
    ktv = [_dot_tn(k_e[ic], v_b[ic]) for ic in chunks]
    ua = [_dot(x[ic].astype(BF16), jnp.concatenate([akv[ic].astype(BF16), a_t[ic]], axis=1)).astype(BF16)
          for ic in chunks]
    rb_ua = [_dot(a_rb[ic], ua[ic]) for ic in chunks]
    bt_ua = [_dot_tn(b_e[ic], ua[ic]) for ic in chunks]
    r_bars = [(r_t[ic] - rb_ua[ic][:, LANES:]).astype(BF16) for ic in chunks]
    y0 = [rkv[ic] - rb_ua[ic][:, :LANES] for ic in chunks]
    y0s = [y_[:c, :] + y_[c:, :] for y_ in y0]
    g_mats = [(jnp.where(eye, w_end[ic], 0.0) - bt_ua[ic][:, LANES:]).astype(BF16) for ic in chunks]
    h_mats = [ktv[ic] - bt_ua[ic][:, :LANES] for ic in chunks]

    p = p_ref[...]
    ys = []
    for ic in range(nc):
        pb = p.astype(BF16)
        y_bd = _dot(r_bars[ic], pb)
        ys.append(y_bd[:c, :] + y_bd[c:, :] + y0s[ic])
        p = _dot(g_mats[ic], pb) + h_mats[ic]
    p_ref[...] = p

    yv = jnp.concatenate(ys, axis=0)
    inv_d = 1.0 / RW_HEAD_DIM
    mean = head_sum(yv) * inv_d
    yc = yv - mean
    var = head_sum(yc * yc) * inv_d
    yn = yc * lax.rsqrt(var + RW_GN_EPS) * gng_ref[...] + gnb_ref[...]
    bonus = head_sum(r * kmod * rk_ref[...]) * v
    o_ref[0] = ((yn + bonus) * gate).astype(o_ref.dtype)


def _wkv(z_rw, mu, w0, w2, a0, a2, g2, k_k, k_a, r_k, gn_g, gn_b, ts):
    b, s, _ = z_rw.shape
    npair = RW_WIDTH // LANES
    lo_blk = (3 * RW_WIDTH) // (2 * LANES)
    row = lambda a: a.reshape(1, -1)

    def col(off):
        return pl.BlockSpec((1, ts, LANES), lambda ib, ip, it: (ib, it, off + ip))

    def vec(off):
        return pl.BlockSpec((1, LANES), lambda ib, ip, it: (0, off + ip))

    in_specs = [
        col(0), col(npair), col(2 * npair),
        pl.BlockSpec((1, ts, 2 * LANES), lambda ib, ip, it: (ib, it, lo_blk)),
        vec(0), vec(npair), vec(2 * npair),
        pl.BlockSpec((1, 2 * LANES), lambda ib, ip, it: (0, lo_blk)),
        vec(0), vec(0), vec(0), vec(0), vec(0), vec(0), vec(0),
        pl.BlockSpec((RW_DECAY_LORA, LANES), lambda ib, ip, it: (0, ip)),
        pl.BlockSpec((RW_ICLR_LORA, LANES), lambda ib, ip, it: (0, ip)),
        pl.BlockSpec((RW_GATE_LORA, LANES), lambda ib, ip, it: (0, ip)),
    ]
    scratch = [
        pltpu.VMEM((LANES, LANES), F32),
        pltpu.VMEM((1, LANES), F32), pltpu.VMEM((1, LANES), F32), pltpu.VMEM((1, LANES), F32),
        pltpu.VMEM((1, 2 * LANES), F32),
    ]
    mu2 = row(mu)
    return pl.pallas_call(
        _wkv_kernel,
        grid=(b, npair, s // ts),
        in_specs=in_specs,
        out_specs=pl.BlockSpec((1, ts, LANES), lambda ib, ip, it: (ib, it, ip)),
        out_shape=jax.ShapeDtypeStruct((b, s, RW_WIDTH), BF16),
        scratch_shapes=scratch,
        compiler_params=_cparams(("parallel", "parallel", "arbitrary")),
        name="wkv",
    )(z_rw, z_rw, z_rw, z_rw, mu2, mu2, mu2, mu2,
      row(w0), row(a0), row(k_k), row(k_a), row(r_k), row(gn_g), row(gn_b),
      w2.astype(BF16), a2.astype(BF16), g2.astype(BF16))


ATTN_ROWS = 16


def _attn_kernel(lq1_ref, lk1_ref, lq2_ref, lk2_ref, q_ref, k_ref, v_ref, subg_ref, o_ref,
                 qs_ref, s_ref, p_ref, m_ref, alpha_ref, acc_ref, *, lambda_init, tk):
    tq = q_ref.shape[1]
    rows = 2 * tq
    qi = pl.program_id(2)
    ngroups = rows // ATTN_ROWS
    ntiles = tk // LANES
    heads = (0, 1)

    lane_h0 = lax.broadcasted_iota(jnp.int32, (tq, LANES), 1) < DA_QK_DIM
    for h in heads:
        q = q_ref[0, :, h * LANES:(h + 1) * LANES]
        zero = jnp.zeros_like(q)
        qsc = (q.astype(F32) * (DA_QK_DIM ** -0.5 * math.log2(math.e))).astype(BF16)
        qs_ref[h] = jnp.concatenate([jnp.where(lane_h0, qsc, zero), jnp.where(lane_h0, zero, qsc)], axis=0)
    m_ref[...] = jnp.full_like(m_ref, -jnp.inf)
    alpha_ref[...] = jnp.zeros_like(alpha_ref)
    acc_ref[...] = jnp.zeros_like(acc_ref)
    p_ref[1] = jnp.zeros((rows, tk), BF16)

    key_chunk = lax.broadcasted_iota(jnp.int32, (ATTN_ROWS, LANES), 1) // CHUNK
    ones = jnp.ones((tk, LANES), BF16)

    def scores(h, j, width=tk):
        koff = pl.multiple_of(j * tk, tk)
        s_ref[h, :, :width] = _dot_nt(qs_ref[h], k_ref[0, pl.ds(koff, width), h * LANES:(h + 1) * LANES])

    def accumulate(h, j, width=tk):
        koff = pl.multiple_of(j * tk, tk)
        v_ext = jnp.concatenate([v_ref[0, pl.ds(koff, width), h * LANES:(h + 1) * LANES], ones[:width]], axis=1)
        alpha = alpha_ref[h]
        acc_ref[h] = jnp.concatenate([alpha, alpha], axis=1) * acc_ref[h] + _dot(p_ref[h, :, :width], v_ext)

    def softmax(h, j, width=tk, first_masked_tile=None):
        def load_scores(g, t):
            r0 = g * ATTN_ROWS
            s = s_ref[h, r0:r0 + ATTN_ROWS, t * LANES:(t + 1) * LANES]
            if first_masked_tile is not None and t >= first_masked_tile:
                q_chunk = (qi * tq + r0 % tq) // CHUNK - j * (tk // CHUNK) - t * (LANES // CHUNK)
                s = jnp.where(key_chunk <= q_chunk, s, -jnp.inf)
            return s

        tiles = range(width // LANES)
        for g in range(ngroups):
            sl = slice(g * ATTN_ROWS, (g + 1) * ATTN_ROWS)
            m_prev = m_ref[h, sl, :]
            tile_max = load_scores(g, 0)
            for t in tiles[1:]:
                tile_max = jnp.maximum(tile_max, load_scores(g, t))
            m_new = jnp.maximum(m_prev, jnp.max(tile_max, axis=-1, keepdims=True))
            alpha_ref[h, sl, :] = jnp.exp2(m_prev - m_new)
            m_ref[h, sl, :] = m_new
        for g in range(ngroups):
            sl = slice(g * ATTN_ROWS, (g + 1) * ATTN_ROWS)
            m_row = m_ref[h, sl, :]
            for t in tiles:
                p_ref[h, sl, t * LANES:(t + 1) * LANES] = jnp.exp2(load_scores(g, t) - m_row).astype(BF16)

    def step(j, width=tk, first_masked_tile=None):
        scores(1, j, width)
        accumulate(1, jnp.maximum(j - 1, 0))
        softmax(0, j, width, first_masked_tile)
        accumulate(0, j, width)
        if first_masked_tile is None:
            scores(0, j + 1)
        softmax(1, j, width, first_masked_tile)

    nfull = (qi * tq + CHUNK) // tk
    scores(0, 0)

    def full_step(j, carry):
        step(j)
        return carry

    lax.fori_loop(0, nfull, full_step, 0)
    q_per_k = tk // tq
    for v in range(q_per_k):
        @pl.when(qi % q_per_k == v)
        def _(v=v):
            width = (v + 1) * tq
            step(nfull, width, v * tq // LANES)
            accumulate(1, nfull, width)

    lam = (jnp.exp(jnp.sum(lq1_ref[...] * lk1_ref[...], axis=-1, keepdims=True))
           - jnp.exp(jnp.sum(lq2_ref[...] * lk2_ref[...], axis=-1, keepdims=True))
           + lambda_init)
    for h in heads:
        acc = acc_ref[h]
        o = acc[:, :DA_V_DIM] / acc[:, DA_V_DIM:]
        o = o[:tq, :] - lam * o[tq:, :]
        o = _rms(o, subg_ref[...], DA_SUBLN_EPS) * (1.0 - lambda_init)
        o_ref[0, :, h * LANES:(h + 1) * LANES] = o.astype(o_ref.dtype)


def _attn(z_da, lq1, lk1, lq2, lk2, sub_g, lambda_init, tq, tk):
    b, s, _ = z_da.shape
    assert tk % tq == 0 and s % tk == 0
    row = lambda a: a.reshape(1, -1)
    lam_spec = pl.BlockSpec((1, DA_QK_DIM), lambda ib, ih, iq: (0, 0))
    kern = functools.partial(_attn_kernel, lambda_init=lambda_init, tk=tk)
    rows = 2 * tq
    pair = 2 * LANES
    npairs = DA_HEADS // 2
    rep = pltpu.VMEM((2, rows, LANES), F32)
    return pl.pallas_call(
        kern,
        grid=(b, npairs, s // tq),
        in_specs=[
            lam_spec, lam_spec, lam_spec, lam_spec,
            pl.BlockSpec((1, tq, pair), lambda ib, ih, iq: (ib, iq, ih)),
            pl.BlockSpec((1, s, pair), lambda ib, ih, iq: (ib, 0, npairs + ih)),
            pl.BlockSpec((1, s, pair), lambda ib, ih, iq: (ib, 0, 2 * npairs + ih)),
            pl.BlockSpec((1, DA_V_DIM), lambda ib, ih, iq: (0, 0)),
        ],
        out_specs=pl.BlockSpec((1, tq, pair), lambda ib, ih, iq: (ib, iq, ih)),
        out_shape=jax.ShapeDtypeStruct((b, s, DA_V_WIDTH), BF16),
        scratch_shapes=[
            pltpu.VMEM((2, rows, LANES), BF16),
            pltpu.VMEM((2, rows, tk), F32),
            pltpu.VMEM((2, rows, tk), BF16),
            rep, rep,
            pltpu.VMEM((2, rows, 2 * DA_V_DIM), F32),
        ],
        compiler_params=_cparams(("parallel", "parallel", "arbitrary")),
        name="attn",
    )(row(lq1), row(lk1), row(lq2), row(lk2), z_da, z_da, z_da, row(sub_g))


def _mix_kernel(x_ref, yrw_ref, yda_ref, grw_ref, gda_ref, wrw_ref, wda_ref, wo_ref, g_ref, o_ref):
    y_rw = _dot(yrw_ref[...], wrw_ref[...])
    y_da = _dot(yda_ref[...], wda_ref[...])
    merged = _sigmoid(grw_ref[...]) * y_rw + _sigmoid(gda_ref[...]) * y_da
    mo = _dot(merged.astype(BF16), wo_ref[...])
    o_ref[...] = x_ref[...] + _rms(mo, g_ref[...], RMS_EPS)


def _mix(x2d, y_rw, y_da, z_g, w_rw, w_da, w_o, g, tm):
    t, d = x2d.shape
    full = lambda shape: pl.BlockSpec(shape, lambda i: (0, 0))
    return pl.pallas_call(
        _mix_kernel,
        grid=(t // tm,),
        in_specs=[
            pl.BlockSpec((tm, d), lambda i: (i, 0)),
            pl.BlockSpec((tm, RW_WIDTH), lambda i: (i, 0)),
            pl.BlockSpec((tm, DA_V_WIDTH), lambda i: (i, 0)),
            pl.BlockSpec((tm, d), lambda i: (i, 0)),
            pl.BlockSpec((tm, d), lambda i: (i, 1)),
            full(w_rw.shape), full(w_da.shape), full(w_o.shape), full((1, d)),
        ],
        out_specs=pl.BlockSpec((tm, d), lambda i: (i, 0)),
        out_shape=jax.ShapeDtypeStruct((t, d), F32),
        compiler_params=_cparams(("parallel",)),
        name="mix",
    )(x2d, y_rw, y_da, z_g, z_g, w_rw, w_da, w_o, g)


def _mlp_kernel(x_ref, gpre_ref, wup_ref, wdn_ref, gpost_ref, o_ref, h_ref, acc_ref):
    j = pl.program_id(1)

    @pl.when(j == 0)
    def _():
        h_ref[...] = _rms(x_ref[...], gpre_ref[...], RMS_EPS).astype(BF16)
        acc_ref[...] = jnp.zeros_like(acc_ref)

    up = _dot(h_ref[...], wup_ref[...])
    act = jnp.square(jnp.maximum(up, 0.0))
    acc_ref[...] += _dot(act.astype(BF16), wdn_ref[...])

    @pl.when(j == pl.num_programs(1) - 1)
    def _():
        o_ref[...] = x_ref[...] + _rms(acc_ref[...], gpost_ref[...], RMS_EPS)


def _mlp(x2d, g_pre, w_up, w_dn, g_post, tm, tf):
    t, d = x2d.shape
    ff = w_up.shape[1]
    return pl.pallas_call(
        _mlp_kernel,
        grid=(t // tm, ff // tf),
        in_specs=[
            pl.BlockSpec((tm, d), lambda i, j: (i, 0)),
            pl.BlockSpec((1, d), lambda i, j: (0, 0)),
            pl.BlockSpec((d, tf), lambda i, j: (0, j)),
            pl.BlockSpec((tf, d), lambda i, j: (j, 0)),
            pl.BlockSpec((1, d), lambda i, j: (0, 0)),
        ],
        out_specs=pl.BlockSpec((tm, d), lambda i, j: (i, 0)),
        out_shape=jax.ShapeDtypeStruct((t, d), F32),
        scratch_shapes=[pltpu.VMEM((tm, d), BF16), pltpu.VMEM((tm, d), F32)],
        compiler_params=_cparams(("parallel", "arbitrary")),
        name="mlp",
    )(x2d, g_pre, w_up, w_dn, g_post)


def _tile(n, want):
    t = min(n, want)
    while n % t:
        t //= 2
    return t


def kernel(x, pre_mix_g, w_in, rw_shift_mu, rw_w0, rw_w2, rw_a0, rw_a2, rw_g2, rw_k_k, rw_k_a, rw_r_k, rw_gn_g, rw_gn_b, w_rw_out, da_lam_q1, da_lam_k1, da_lam_q2, da_lam_k2, da_sub_g, w_da_out, w_o, post_mix_g, pre_mlp_g, w_up, w_down, post_mlp_g):
    b, s, d = x.shape
    depth = w_in.shape[0]
    t = b * s
    tm = _tile(t, 512)
    for l in range(depth):
        lambda_init = 0.8 - 0.6 * math.exp(-0.3 * l)
        x2d = x.reshape(t, d)
        g_pre = pre_mix_g[l].reshape(1, d)
        z_rw, z_da, z_g = _inproj(x2d, g_pre, w_in[l].astype(BF16), _tile(t, 256))

        y_rw = _wkv(z_rw.reshape(b, s, RW_IN), rw_shift_mu[l], rw_w0[l], rw_w2[l], rw_a0[l],
                    rw_a2[l], rw_g2[l], rw_k_k[l], rw_k_a[l], rw_r_k[l], rw_gn_g[l], rw_gn_b[l],
                    _tile(s, 1024))
        y_da = _attn(z_da.reshape(b, s, DA_IN), da_lam_q1[l], da_lam_k1[l], da_lam_q2[l],
                     da_lam_k2[l], da_sub_g[l], lambda_init, _tile(s, 512), _tile(s, 1024))

        x1 = _mix(x2d, y_rw.reshape(t, RW_WIDTH), y_da.reshape(t, DA_V_WIDTH), z_g,
                  w_rw_out[l].astype(BF16), w_da_out[l].astype(BF16), w_o[l].astype(BF16),
                  post_mix_g[l].reshape(1, d), tm)
        x2 = _mlp(x1, pre_mlp_g[l].reshape(1, d), w_up[l].astype(BF16), w_down[l].astype(BF16),
                  post_mlp_g[l].reshape(1, d), _tile(t, 1024), 1024)
        x = x2.reshape(b, s, d)
    return x
```

```python
import functools
import math

import jax
import jax.numpy as jnp
from jax import lax
from jax.experimental import pallas as pl
from jax.experimental.pallas import tpu as pltpu

F32 = jnp.float32
BF16 = jnp.bfloat16

RMS_EPS = 1e-6
CHUNK = 64
RW_HEADS = 8
RW_HEAD_DIM = 64
RW_WIDTH = RW_HEADS * RW_HEAD_DIM
RW_DECAY_LORA = 64
RW_ICLR_LORA = 64
RW_GATE_LORA = 128
RW_GN_EPS = 1e-5 * RW_HEAD_DIM
RW_IN = 3 * RW_WIDTH + RW_DECAY_LORA + RW_ICLR_LORA + RW_GATE_LORA
DA_HEADS = 4
DA_QK_DIM = 64
DA_V_DIM = 2 * DA_QK_DIM
DA_QK_WIDTH = DA_HEADS * 2 * DA_QK_DIM
DA_V_WIDTH = DA_HEADS * DA_V_DIM
DA_IN = 2 * DA_QK_WIDTH + DA_V_WIDTH
DA_SUBLN_EPS = 1e-5

LANES = 128
WKV_CHUNK = 64
VMEM_LIMIT = 56 * 1024 * 1024


def _cparams(semantics):
    return pltpu.CompilerParams(dimension_semantics=semantics, vmem_limit_bytes=VMEM_LIMIT)


def _rms(x, g, eps):
    return x * lax.rsqrt(jnp.mean(x * x, axis=-1, keepdims=True) + eps) * g


def _dot(a, b):
    return jnp.dot(a, b, preferred_element_type=F32)


def _dot_nt(a, b):
    return lax.dot_general(a, b, (((1,), (1,)), ((), ())), preferred_element_type=F32)


def _dot_tn(a, b):
    return lax.dot_general(a, b, (((0,), (0,)), ((), ())), preferred_element_type=F32)


def _inproj_kernel(x_ref, g_ref, w_ref, orw_ref, oda_ref, og_ref):
    h = _rms(x_ref[...], g_ref[...], RMS_EPS).astype(BF16)
    o1, o2 = RW_IN, RW_IN + DA_IN
    orw_ref[...] = _dot(h, w_ref[:, :o1])
    oda_ref[...] = _dot(h, w_ref[:, o1:o2]).astype(oda_ref.dtype)
    og_ref[...] = _dot(h, w_ref[:, o2:])


def _inproj(x2d, g, w_bf16, tm):
    t, d = x2d.shape
    n = w_bf16.shape[1]
    n_gate = n - RW_IN - DA_IN
    return pl.pallas_call(
        _inproj_kernel,
        grid=(t // tm,),
        in_specs=[
            pl.BlockSpec((tm, d), lambda i: (i, 0)),
            pl.BlockSpec((1, d), lambda i: (0, 0)),
            pl.BlockSpec((d, n), lambda i: (0, 0)),
        ],
        out_specs=[
            pl.BlockSpec((tm, RW_IN), lambda i: (i, 0)),
            pl.BlockSpec((tm, DA_IN), lambda i: (i, 0)),
            pl.BlockSpec((tm, n_gate), lambda i: (i, 0)),
        ],
        out_shape=[
            jax.ShapeDtypeStruct((t, RW_IN), F32),
            jax.ShapeDtypeStruct((t, DA_IN), BF16),
            jax.ShapeDtypeStruct((t, n_gate), F32),
        ],
        compiler_params=_cparams(("parallel",)),
        name="inproj",
    )(x2d, g, w_bf16)


def _sigmoid(x):
    return 1.0 / (1.0 + jnp.exp(-x))


def _split_bf16(x, n):
    parts = []
    for _ in range(n - 1):
        hi = x.astype(BF16)
        parts.append(hi)
        x = x - hi.astype(F32)
    parts.append(x.astype(BF16))
    return parts


def _inv_dot(a, b):
    return _dot(a.astype(BF16), b.astype(BF16))


def _wkv_kernel(r_ref, k_ref, v_ref, lo_ref, mur_ref, muk_ref, muv_ref, mulo_ref,
                w0_ref, a0_ref, kk_ref, ka_ref, rk_ref, gng_ref, gnb_ref,
                w2_ref, a2_ref, g2_ref, o_ref,
                p_ref, cr_ref, ck_ref, cv_ref, clo_ref):
    ts = r_ref.shape[1]
    nc = ts // WKV_CHUNK
    c = WKV_CHUNK

    @pl.when(pl.program_id(2) == 0)
    def _():
        p_ref[...] = jnp.zeros_like(p_ref)
        cr_ref[...] = jnp.zeros_like(cr_ref)
        ck_ref[...] = jnp.zeros_like(ck_ref)
        cv_ref[...] = jnp.zeros_like(cv_ref)
        clo_ref[...] = jnp.zeros_like(clo_ref)

    def shifted(z, carry_ref, mu):
        prev = pltpu.roll(z, 1, axis=0)
        row = lax.broadcasted_iota(jnp.int32, z.shape, 0)
        prev = jnp.where(row == 0, carry_ref[...], prev)
        carry_ref[...] = z[ts - 1:ts, :]
        return z + (prev - z) * mu

    r = shifted(r_ref[0], cr_ref, mur_ref[...])
    k = shifted(k_ref[0], ck_ref, muk_ref[...])
    v = shifted(v_ref[0], cv_ref, muv_ref[...])
    lo = shifted(lo_ref[0], clo_ref, mulo_ref[...])
    xwa = lo[:, :LANES]
    xg = lo[:, LANES:]

    zeros64 = jnp.zeros((RW_DECAY_LORA, LANES), BF16)
    w2p = jnp.concatenate([w2_ref[...], zeros64], axis=0)
    a2p = jnp.concatenate([zeros64, a2_ref[...]], axis=0)
    wl = _dot(jnp.tanh(xwa).astype(BF16), w2p)
    al = _dot(xwa.astype(BF16), a2p)
    gate = _dot(_sigmoid(xg).astype(BF16), g2_ref[...])

    y = -(w0_ref[...] + wl)
    softplus = jnp.maximum(y, 0.0) + jnp.log(1.0 + jnp.exp(-jnp.abs(y)))
    lw = -jnp.exp(-softplus - 0.5)
    a = _sigmoid(a0_ref[...] + al)

    lane = lax.broadcasted_iota(jnp.int32, (LANES, LANES), 1)
    rowi = lax.broadcasted_iota(jnp.int32, (LANES, LANES), 0)
    same_head = (lane // RW_HEAD_DIM) == (rowi // RW_HEAD_DIM)
    head_ones = jnp.where(same_head, 1.0, 0.0).astype(BF16)

    def head_sum(x):
        hi, lo = _split_bf16(x, 2)
        return _dot(hi, head_ones) + _dot(lo, head_ones)

    kk = k * kk_ref[...]
    kk = kk / jnp.maximum(jnp.sqrt(head_sum(kk * kk)), 1e-12)
    kmod = k * (1.0 + (a - 1.0) * ka_ref[...])

    bb = kk * a

    ri = lax.broadcasted_iota(jnp.int32, (c, c), 0)
    ci = lax.broadcasted_iota(jnp.int32, (c, c), 1)
    tri_incl = jnp.where(ci <= ri, 1.0, 0.0).astype(BF16)
    lane_h0 = lax.broadcasted_iota(jnp.int32, (c, LANES), 1) < RW_HEAD_DIM
    strict = lane < rowi
    incl = lane <= rowi
    eye = lane == rowi
    eye_f = jnp.where(eye, 1.0, 0.0)

    def bd(x):
        return jnp.concatenate([jnp.where(lane_h0, x, 0.0), jnp.where(lane_h0, 0.0, x)], axis=0)

    def level_mask(m):
        return ((rowi // (2 * m)) == (lane // (2 * m))) & ((rowi % (2 * m)) >= m) & ((lane % (2 * m)) < m)

    level_masks = {m: level_mask(m) for m in (1, 2, 4, 8, 16, 32)}

    lw_parts = jnp.concatenate(_split_bf16(lw, 3), axis=1)
    cums, cends = [], []
    for ic in range(nc):
        cum3 = _dot(tri_incl, lw_parts[ic * c:(ic + 1) * c, :])
        cum_c = cum3[:, :LANES] + cum3[:, LANES:2 * LANES] + cum3[:, 2 * LANES:]
        cums.append(cum_c)
        cends.append(jnp.broadcast_to(cum_c[c - 1:c, :], (c, LANES)))
    cum = jnp.concatenate(cums, axis=0)
    cend = jnp.concatenate(cends, axis=0)
    einv = jnp.exp(-cum)
    eend = jnp.exp(cend - cum)
    a_til = kk * jnp.exp(cum - lw)
    r_til = r * jnp.exp(cum)
    k_til = kmod * einv
    b_til = bb * einv
    k_end = kmod * eend
    b_end = bb * eend

    chunks = range(nc)
    rows_of = lambda x, ic: x[ic * c:(ic + 1) * c]
    a_t = [bd(rows_of(a_til, ic)).astype(BF16) for ic in chunks]
    r_t = [bd(rows_of(r_til, ic)) for ic in chunks]
    k_t = [bd(rows_of(k_til, ic)).astype(BF16) for ic in chunks]
    b_t = [bd(rows_of(b_til, ic)).astype(BF16) for ic in chunks]
    k_e = [bd(rows_of(k_end, ic)).astype(BF16) for ic in chunks]
    b_e = [bd(rows_of(b_end, ic)).astype(BF16) for ic in chunks]
    v_b = [bd(rows_of(v, ic)).astype(BF16) for ic in chunks]
    w_end = [jnp.exp(cend[ic * c:ic * c + 1, :]) for ic in chunks]

    sc = [_dot_nt(jnp.concatenate([a_t[ic], r_t[ic].astype(BF16)], axis=0),
                  jnp.concatenate([k_t[ic], b_t[ic]], axis=0)) for ic in chunks]
    l_ab = [jnp.where(strict, s_[:LANES, LANES:], 0.0) for s_ in sc]
    a_ak = [jnp.where(strict, s_[:LANES, :LANES], 0.0).astype(BF16) for s_ in sc]
    a_rk = [jnp.where(incl, s_[LANES:, :LANES], 0.0).astype(BF16) for s_ in sc]
    a_rb = [jnp.where(incl, s_[LANES:, LANES:], 0.0).astype(BF16) for s_ in sc]

    x = [eye_f - jnp.where(level_masks[1], l_, 0.0) for l_ in l_ab]
    for m in (2, 4, 8, 16, 32):
        xc = [_inv_dot(x[ic], jnp.where(level_masks[m], l_ab[ic], 0.0)) for ic in chunks]
        x = [x[ic] - _inv_dot(xc[ic], x[ic]) for ic in chunks]

    akv = [_dot(a_ak[ic], v_b[ic]) for ic in chunks]
    rkv = [_dot(a_rk[ic], v_b[ic]) for ic in chunks]
    ktv = [_dot_tn(k_e[ic], v_b[ic]) for ic in chunks]
    ua = [_dot(x[ic].astype(BF16), jnp.concatenate([akv[ic].astype(BF16), a_t[ic]], axis=1)).astype(BF16)
          for ic in chunks]
    rb_ua = [_dot(a_rb[ic], ua[ic]) for ic in chunks]
    bt_ua = [_dot_tn(b_e[ic], ua[ic]) for ic in chunks]
    r_bars = [(r_t[ic] - rb_ua[ic][:, LANES:]).astype(BF16) for ic in chunks]
    y0 = [rkv[ic] - rb_ua[ic][:, :LANES] for ic in chunks]
    y0s = [y_[:c, :] + y_[c:, :] for y_ in y0]
    g_mats = [(jnp.where(eye, w_end[ic], 0.0) - bt_ua[ic][:, LANES:]).astype(BF16) for ic in chunks]
    h_mats = [ktv[ic] - bt_ua[ic][:, :LANES] for ic in chunks]

    pairs = range(nc // 2)
    gh2 = [_dot(g_mats[2 * k + 1], jnp.concatenate([g_mats[2 * k], h_mats[2 * k].astype(BF16)], axis=1))
           for k in pairs]
    g2 = [gh[:, :LANES].astype(BF16) for gh in gh2]
    h2 = [gh2[k][:, LANES:] + h_mats[2 * k + 1] for k in pairs]
    p = p_ref[...]
    p_even = []
    for k in pairs:
        pb = p.astype(BF16)
        p_even.append(pb)
        p = _dot(g2[k], pb) + h2[k]
    p_ref[...] = p
    p_odd = [(_dot(g_mats[2 * k], p_even[k]) + h_mats[2 * k]).astype(BF16) for k in pairs]
    states = [p_even[ic // 2] if ic % 2 == 0 else p_odd[ic // 2] for ic in chunks]
    y_bd = [_dot(r_bars[ic], states[ic]) for ic in chunks]
    ys = [y_bd[ic][:c, :] + y_bd[ic][c:, :] + y0s[ic] for ic in chunks]

    yv = jnp.concatenate(ys, axis=0)
    inv_d = 1.0 / RW_HEAD_DIM
    mean = head_sum(yv) * inv_d
    yc = yv - mean
    var = head_sum(yc * yc) * inv_d
    yn = yc * lax.rsqrt(var + RW_GN_EPS) * gng_ref[...] + gnb_ref[...]
    bonus = head_sum(r * kmod * rk_ref[...]) * v
    o_ref[0] = ((yn + bonus) * gate).astype(o_ref.dtype)


def _wkv(z_rw, mu, w0, w2, a0, a2, g2, k_k, k_a, r_k, gn_g, gn_b, ts):
    b, s, _ = z_rw.shape
    assert s % ts == 0 and ts % (2 * WKV_CHUNK) == 0
    npair = RW_WIDTH // LANES
    lo_blk = (3 * RW_WIDTH) // (2 * LANES)
    row = lambda a: a.reshape(1, -1)

    def col(off):
        return pl.BlockSpec((1, ts, LANES), lambda ib, ip, it: (ib, it, off + ip))

    def vec(off):
        return pl.BlockSpec((1, LANES), lambda ib, ip, it: (0, off + ip))

    in_specs = [
        col(0), col(npair), col(2 * npair),
        pl.BlockSpec((1, ts, 2 * LANES), lambda ib, ip, it: (ib, it, lo_blk)),
        vec(0), vec(npair), vec(2 * npair),
        pl.BlockSpec((1, 2 * LANES), lambda ib, ip, it: (0, lo_blk)),
        vec(0), vec(0), vec(0), vec(0), vec(0), vec(0), vec(0),
        pl.BlockSpec((RW_DECAY_LORA, LANES), lambda ib, ip, it: (0, ip)),
        pl.BlockSpec((RW_ICLR_LORA, LANES), lambda ib, ip, it: (0, ip)),
        pl.BlockSpec((RW_GATE_LORA, LANES), lambda ib, ip, it: (0, ip)),
    ]
    scratch = [
        pltpu.VMEM((LANES, LANES), F32),
        pltpu.VMEM((1, LANES), F32), pltpu.VMEM((1, LANES), F32), pltpu.VMEM((1, LANES), F32),
        pltpu.VMEM((1, 2 * LANES), F32),
    ]
    mu2 = row(mu)
    return pl.pallas_call(
        _wkv_kernel,
        grid=(b, npair, s // ts),
        in_specs=in_specs,
        out_specs=pl.BlockSpec((1, ts, LANES), lambda ib, ip, it: (ib, it, ip)),
        out_shape=jax.ShapeDtypeStruct((b, s, RW_WIDTH), BF16),
        scratch_shapes=scratch,
        compiler_params=_cparams(("parallel", "parallel", "arbitrary")),
        name="wkv",
    )(z_rw, z_rw, z_rw, z_rw, mu2, mu2, mu2, mu2,
      row(w0), row(a0), row(k_k), row(k_a), row(r_k), row(gn_g), row(gn_b),
      w2.astype(BF16), a2.astype(BF16), g2.astype(BF16))


ATTN_ROWS = 32


def _attn_kernel(lq1_ref, lk1_ref, lq2_ref, lk2_ref, q_ref, k_ref, v_ref, subg_ref, o_ref,
                 qs_ref, s_ref, p_ref, m_ref, alpha_ref, acc_ref, *, lambda_init, tk):
    tq = q_ref.shape[1]
    rows = 2 * tq
    qi = pl.program_id(2)
    ngroups = rows // ATTN_ROWS
    ntiles = tk // LANES
    heads = (0, 1)

    lane_h0 = lax.broadcasted_iota(jnp.int32, (tq, LANES), 1) < DA_QK_DIM
    for h in heads:
        q = q_ref[0, :, h * LANES:(h + 1) * LANES]
        zero = jnp.zeros_like(q)
        qsc = (q.astype(F32) * (DA_QK_DIM ** -0.5 * math.log2(math.e))).astype(BF16)
        qs_ref[h] = jnp.concatenate([jnp.where(lane_h0, qsc, zero), jnp.where(lane_h0, zero, qsc)], axis=0)
    m_ref[...] = jnp.full_like(m_ref, -jnp.inf)
    alpha_ref[...] = jnp.zeros_like(alpha_ref)
    acc_ref[...] = jnp.zeros_like(acc_ref)
    p_ref[1] = jnp.zeros((rows, tk), BF16)

    key_chunk = lax.broadcasted_iota(jnp.int32, (ATTN_ROWS, LANES), 1) // CHUNK
    ones = jnp.ones((tk, LANES), BF16)

    def scores(h, j, width=tk):
        koff = pl.multiple_of(j * tk, tk)
        s_ref[h, :, :width] = _dot_nt(qs_ref[h], k_ref[0, pl.ds(koff, width), h * LANES:(h + 1) * LANES])

    def accumulate(h, j, width=tk):
        koff = pl.multiple_of(j * tk, tk)
        v_ext = jnp.concatenate([v_ref[0, pl.ds(koff, width), h * LANES:(h + 1) * LANES], ones[:width]], axis=1)
        alpha = alpha_ref[h]
        acc_ref[h] = jnp.concatenate([alpha, alpha], axis=1) * acc_ref[h] + _dot(p_ref[h, :, :width], v_ext)

    def softmax(h, j, width=tk, first_masked_tile=None):
        def load_scores(g, t):
            r0 = g * ATTN_ROWS
            s = s_ref[h, r0:r0 + ATTN_ROWS, t * LANES:(t + 1) * LANES]
            if first_masked_tile is not None and t >= first_masked_tile:
                q_chunk = (qi * tq + r0 % tq) // CHUNK - j * (tk // CHUNK) - t * (LANES // CHUNK)
                s = jnp.where(key_chunk <= q_chunk, s, -jnp.inf)
            return s

        tiles = range(width // LANES)
        for g in range(ngroups):
            sl = slice(g * ATTN_ROWS, (g + 1) * ATTN_ROWS)
            m_prev = m_ref[h, sl, :]
            tile_max = load_scores(g, 0)
            for t in tiles[1:]:
                tile_max = jnp.maximum(tile_max, load_scores(g, t))
            m_new = jnp.maximum(m_prev, jnp.max(tile_max, axis=-1, keepdims=True))
            alpha_ref[h, sl, :] = jnp.exp2(m_prev - m_new)
            m_ref[h, sl, :] = m_new
        for g in range(ngroups):
            sl = slice(g * ATTN_ROWS, (g + 1) * ATTN_ROWS)
            m_row = m_ref[h, sl, :]
            for t in tiles:
                p_ref[h, sl, t * LANES:(t + 1) * LANES] = jnp.exp2(load_scores(g, t) - m_row).astype(BF16)

    def step(j, width=tk, first_masked_tile=None):
        scores(1, j, width)
        accumulate(1, jnp.maximum(j - 1, 0))
        softmax(0, j, width, first_masked_tile)
        accumulate(0, j, width)
        if first_masked_tile is None:
            scores(0, j + 1)
        softmax(1, j, width, first_masked_tile)

    nfull = (qi * tq + CHUNK) // tk
    scores(0, 0)

    def full_step(j, carry):
        step(j)
        return carry

    lax.fori_loop(0, nfull, full_step, 0)
    q_per_k = tk // tq
    for v in range(q_per_k):
        @pl.when(qi % q_per_k == v)
        def _(v=v):
            width = (v + 1) * tq
            step(nfull, width, v * tq // LANES)
            accumulate(1, nfull, width)

    lam = (jnp.exp(jnp.sum(lq1_ref[...] * lk1_ref[...], axis=-1, keepdims=True))
           - jnp.exp(jnp.sum(lq2_ref[...] * lk2_ref[...], axis=-1, keepdims=True))
           + lambda_init)
    for h in heads:
        acc = acc_ref[h]
        o = acc[:, :DA_V_DIM] / acc[:, DA_V_DIM:]
        o = o[:tq, :] - lam * o[tq:, :]
        o = _rms(o, subg_ref[...], DA_SUBLN_EPS) * (1.0 - lambda_init)
        o_ref[0, :, h * LANES:(h + 1) * LANES] = o.astype(o_ref.dtype)


def _attn(z_da, lq1, lk1, lq2, lk2, sub_g, lambda_init, tq, tk):
    b, s, _ = z_da.shape
    assert tk % tq == 0 and s % tk == 0
    row = lambda a: a.reshape(1, -1)
    lam_spec = pl.BlockSpec((1, DA_QK_DIM), lambda ib, ih, iq: (0, 0))
    kern = functools.partial(_attn_kernel, lambda_init=lambda_init, tk=tk)
    rows = 2 * tq
    pair = 2 * LANES
    npairs = DA_HEADS // 2
    rep = pltpu.VMEM((2, rows, LANES), F32)
    return pl.pallas_call(
        kern,
        grid=(b, npairs, s // tq),
        in_specs=[
            lam_spec, lam_spec, lam_spec, lam_spec,
            pl.BlockSpec((1, tq, pair), lambda ib, ih, iq: (ib, iq, ih)),
            pl.BlockSpec((1, s, pair), lambda ib, ih, iq: (ib, 0, npairs + ih)),
            pl.BlockSpec((1, s, pair), lambda ib, ih, iq: (ib, 0, 2 * npairs + ih)),
            pl.BlockSpec((1, DA_V_DIM), lambda ib, ih, iq: (0, 0)),
        ],
        out_specs=pl.BlockSpec((1, tq, pair), lambda ib, ih, iq: (ib, iq, ih)),
        out_shape=jax.ShapeDtypeStruct((b, s, DA_V_WIDTH), BF16),
        scratch_shapes=[
            pltpu.VMEM((2, rows, LANES), BF16),
            pltpu.VMEM((2, rows, tk), F32),
            pltpu.VMEM((2, rows, tk), BF16),
            rep, rep,
            pltpu.VMEM((2, rows, 2 * DA_V_DIM), F32),
        ],
        compiler_params=_cparams(("parallel", "parallel", "arbitrary")),
        name="attn",
    )(row(lq1), row(lk1), row(lq2), row(lk2), z_da, z_da, z_da, row(sub_g))


def _mix_kernel(x_ref, yrw_ref, yda_ref, grw_ref, gda_ref, wrw_ref, wda_ref, wo_ref, g_ref, o_ref):
    y_rw = _dot(yrw_ref[...], wrw_ref[...])
    y_da = _dot(yda_ref[...], wda_ref[...])
    merged = _sigmoid(grw_ref[...]) * y_rw + _sigmoid(gda_ref[...]) * y_da
    mo = _dot(merged.astype(BF16), wo_ref[...])
    o_ref[...] = x_ref[...] + _rms(mo, g_ref[...], RMS_EPS)


def _mix(x2d, y_rw, y_da, z_g, w_rw, w_da, w_o, g, tm):
    t, d = x2d.shape
    full = lambda shape: pl.BlockSpec(shape, lambda i: (0, 0))
    return pl.pallas_call(
        _mix_kernel,
        grid=(t // tm,),
        in_specs=[
            pl.BlockSpec((tm, d), lambda i: (i, 0)),
            pl.BlockSpec((tm, RW_WIDTH), lambda i: (i, 0)),
            pl.BlockSpec((tm, DA_V_WIDTH), lambda i: (i, 0)),
            pl.BlockSpec((tm, d), lambda i: (i, 0)),
            pl.BlockSpec((tm, d), lambda i: (i, 1)),
            full(w_rw.shape), full(w_da.shape), full(w_o.shape), full((1, d)),
        ],
        out_specs=pl.BlockSpec((tm, d), lambda i: (i, 0)),
        out_shape=jax.ShapeDtypeStruct((t, d), F32),
        compiler_params=_cparams(("parallel",)),
        name="mix",
    )(x2d, y_rw, y_da, z_g, z_g, w_rw, w_da, w_o, g)


def _mlp_kernel(x_ref, gpre_ref, wup_ref, wdn_ref, gpost_ref, o_ref, h_ref, acc_ref):
    j = pl.program_id(1)

    @pl.when(j == 0)
    def _():
        h_ref[...] = _rms(x_ref[...], gpre_ref[...], RMS_EPS).astype(BF16)
        acc_ref[...] = jnp.zeros_like(acc_ref)

    up = _dot(h_ref[...], wup_ref[...])
    act = jnp.square(jnp.maximum(up, 0.0))
    acc_ref[...] += _dot(act.astype(BF16), wdn_ref[...])

    @pl.when(j == pl.num_programs(1) - 1)
    def _():
        o_ref[...] = x_ref[...] + _rms(acc_ref[...], gpost_ref[...], RMS_EPS)


def _mlp(x2d, g_pre, w_up, w_dn, g_post, tm, tf):
    t, d = x2d.shape
    ff = w_up.shape[1]
    return pl.pallas_call(
        _mlp_kernel,
        grid=(t // tm, ff // tf),
        in_specs=[
            pl.BlockSpec((tm, d), lambda i, j: (i, 0)),
            pl.BlockSpec((1, d), lambda i, j: (0, 0)),
            pl.BlockSpec((d, tf), lambda i, j: (0, j)),
            pl.BlockSpec((tf, d), lambda i, j: (j, 0)),
            pl.BlockSpec((1, d), lambda i, j: (0, 0)),
        ],
        out_specs=pl.BlockSpec((tm, d), lambda i, j: (i, 0)),
        out_shape=jax.ShapeDtypeStruct((t, d), F32),
        scratch_shapes=[pltpu.VMEM((tm, d), BF16), pltpu.VMEM((tm, d), F32)],
        compiler_params=_cparams(("parallel", "arbitrary")),
        name="mlp",
    )(x2d, g_pre, w_up, w_dn, g_post)


def _tile(n, want):
    t = min(n, want)
    while n % t:
        t //= 2
    return t


def kernel(x, pre_mix_g, w_in, rw_shift_mu, rw_w0, rw_w2, rw_a0, rw_a2, rw_g2, rw_k_k, rw_k_a, rw_r_k, rw_gn_g, rw_gn_b, w_rw_out, da_lam_q1, da_lam_k1, da_lam_q2, da_lam_k2, da_sub_g, w_da_out, w_o, post_mix_g, pre_mlp_g, w_up, w_down, post_mlp_g):
    b, s, d = x.shape
    depth = w_in.shape[0]
    t = b * s
    tm = _tile(t, 512)
    for l in range(depth):
        lambda_init = 0.8 - 0.6 * math.exp(-0.3 * l)
        x2d = x.reshape(t, d)
        g_pre = pre_mix_g[l].reshape(1, d)
        z_rw, z_da, z_g = _inproj(x2d, g_pre, w_in[l].astype(BF16), _tile(t, 256))

        y_rw = _wkv(z_rw.reshape(b, s, RW_IN), rw_shift_mu[l], rw_w0[l], rw_w2[l], rw_a0[l],
                    rw_a2[l], rw_g2[l], rw_k_k[l], rw_k_a[l], rw_r_k[l], rw_gn_g[l], rw_gn_b[l],
                    _tile(s, 1024))
        y_da = _attn(z_da.reshape(b, s, DA_IN), da_lam_q1[l], da_lam_k1[l], da_lam_q2[l],
                     da_lam_k2[l], da_sub_g[l], lambda_init, _tile(s, 512), _tile(s, 1024))

        x1 = _mix(x2d, y_rw.reshape(t, RW_WIDTH), y_da.reshape(t, DA_V_WIDTH), z_g,
                  w_rw_out[l].astype(BF16), w_da_out[l].astype(BF16), w_o[l].astype(BF16),
                  post_mix_g[l].reshape(1, d), tm)
        x2 = _mlp(x1, pre_mlp_g[l].reshape(1, d), w_up[l].astype(BF16), w_down[l].astype(BF16),
                  post_mlp_g[l].reshape(1, d), _tile(t, 1024), 1024)
        x = x2.reshape(b, s, d)
    return x
```

```python
import functools
import math
from typing import NamedTuple

import jax
import jax.numpy as jnp
from jax import lax
from jax.experimental import pallas as pl
from jax.experimental.pallas import tpu as pltpu

F32 = jnp.float32
BF16 = jnp.bfloat16

RMS_EPS = 1e-6
CHUNK = 64
RW_HEADS = 8
RW_HEAD_DIM = 64
RW_WIDTH = RW_HEADS * RW_HEAD_DIM
RW_DECAY_LORA = 64
RW_ICLR_LORA = 64
RW_GATE_LORA = 128
RW_GN_EPS = 1e-5 * RW_HEAD_DIM
RW_IN = 3 * RW_WIDTH + RW_DECAY_LORA + RW_ICLR_LORA + RW_GATE_LORA
DA_HEADS = 4
DA_QK_DIM = 64
DA_V_DIM = 2 * DA_QK_DIM
DA_QK_WIDTH = DA_HEADS * 2 * DA_QK_DIM
DA_V_WIDTH = DA_HEADS * DA_V_DIM
DA_IN = 2 * DA_QK_WIDTH + DA_V_WIDTH
DA_SUBLN_EPS = 1e-5

LANES = 128
WKV_CHUNK = 64
VMEM_LIMIT = 56 * 1024 * 1024


def _cparams(semantics):
    return pltpu.CompilerParams(dimension_semantics=semantics, vmem_limit_bytes=VMEM_LIMIT)


def _rms(x, g, eps):
    return x * lax.rsqrt(jnp.mean(x * x, axis=-1, keepdims=True) + eps) * g


def _dot(a, b):
    return jnp.dot(a, b, preferred_element_type=F32)


def _dot_nt(a, b):
    return lax.dot_general(a, b, (((1,), (1,)), ((), ())), preferred_element_type=F32)


def _dot_tn(a, b):
    return lax.dot_general(a, b, (((0,), (0,)), ((), ())), preferred_element_type=F32)


def _inproj_kernel(x_ref, g_ref, w_ref, orw_ref, oda_ref, og_ref):
    h = _rms(x_ref[...], g_ref[...], RMS_EPS).astype(BF16)
    o1, o2 = RW_IN, RW_IN + DA_IN
    orw_ref[...] = _dot(h, w_ref[:, :o1])
    oda_ref[...] = _dot(h, w_ref[:, o1:o2]).astype(oda_ref.dtype)
    og_ref[...] = _dot(h, w_ref[:, o2:])


def _inproj(x2d, g, w_bf16, tm):
    t, d = x2d.shape
    n = w_bf16.shape[1]
    n_gate = n - RW_IN - DA_IN
    return pl.pallas_call(
        _inproj_kernel,
        grid=(t // tm,),
        in_specs=[
            pl.BlockSpec((tm, d), lambda i: (i, 0)),
            pl.BlockSpec((1, d), lambda i: (0, 0)),
            pl.BlockSpec((d, n), lambda i: (0, 0)),
        ],
        out_specs=[
            pl.BlockSpec((tm, RW_IN), lambda i: (i, 0)),
            pl.BlockSpec((tm, DA_IN), lambda i: (i, 0)),
            pl.BlockSpec((tm, n_gate), lambda i: (i, 0)),
        ],
        out_shape=[
            jax.ShapeDtypeStruct((t, RW_IN), F32),
            jax.ShapeDtypeStruct((t, DA_IN), BF16),
            jax.ShapeDtypeStruct((t, n_gate), F32),
        ],
        compiler_params=_cparams(("parallel",)),
        name="inproj",
    )(x2d, g, w_bf16)


def _sigmoid(x):
    return 1.0 / (1.0 + jnp.exp(-x))


def _split_bf16(x, n):
    parts = []
    for _ in range(n - 1):
        hi = x.astype(BF16)
        parts.append(hi)
        x = x - hi.astype(F32)
    parts.append(x.astype(BF16))
    return parts


def _wkv_kernel(r_ref, k_ref, v_ref, lo_ref, mur_ref, muk_ref, muv_ref, mulo_ref,
                w0_ref, a0_ref, kk_ref, ka_ref, rk_ref, gng_ref, gnb_ref,
                w2_ref, a2_ref, g2_ref, o_ref,
                p_ref, cr_ref, ck_ref, cv_ref, clo_ref):
    ts = r_ref.shape[1]
    nc = ts // WKV_CHUNK
    c = WKV_CHUNK

    @pl.when(pl.program_id(2) == 0)
    def _():
        p_ref[...] = jnp.zeros_like(p_ref)
        cr_ref[...] = jnp.zeros_like(cr_ref)
        ck_ref[...] = jnp.zeros_like(ck_ref)
        cv_ref[...] = jnp.zeros_like(cv_ref)
        clo_ref[...] = jnp.zeros_like(clo_ref)

    def shifted(z, carry_ref, mu):
        prev = pltpu.roll(z, 1, axis=0)
        row = lax.broadcasted_iota(jnp.int32, z.shape, 0)
        prev = jnp.where(row == 0, carry_ref[...], prev)
        carry_ref[...] = z[ts - 1:ts, :]
        return z + (prev - z) * mu

    r = shifted(r_ref[0], cr_ref, mur_ref[...])
    k = shifted(k_ref[0], ck_ref, muk_ref[...])
    v = shifted(v_ref[0], cv_ref, muv_ref[...])
    lo = shifted(lo_ref[0], clo_ref, mulo_ref[...])
    xwa = lo[:, :LANES]
    xg = lo[:, LANES:]

    zeros64 = jnp.zeros((RW_DECAY_LORA, LANES), BF16)
    w2p = jnp.concatenate([w2_ref[...], zeros64], axis=0)
    a2p = jnp.concatenate([zeros64, a2_ref[...]], axis=0)
    wl = _dot(jnp.tanh(xwa).astype(BF16), w2p)
    al = _dot(xwa.astype(BF16), a2p)
    gate = _dot(_sigmoid(xg).astype(BF16), g2_ref[...])

    y = -(w0_ref[...] + wl)
    softplus = jnp.maximum(y, 0.0) + jnp.log(1.0 + jnp.exp(-jnp.abs(y)))
    lw = -jnp.exp(-softplus - 0.5)
    a = _sigmoid(a0_ref[...] + al)

    lane = lax.broadcasted_iota(jnp.int32, (LANES, LANES), 1)
    rowi = lax.broadcasted_iota(jnp.int32, (LANES, LANES), 0)
    same_head = (lane // RW_HEAD_DIM) == (rowi // RW_HEAD_DIM)
    head_ones = jnp.where(same_head, 1.0, 0.0).astype(BF16)

    def head_sum(x):
        hi, lo = _split_bf16(x, 2)
        return _dot(hi, head_ones) + _dot(lo, head_ones)

    kk = k * kk_ref[...]
    kk = kk / jnp.maximum(jnp.sqrt(head_sum(kk * kk)), 1e-12)
    kmod = k * (1.0 + (a - 1.0) * ka_ref[...])

    bb = kk * a

    ri = lax.broadcasted_iota(jnp.int32, (c, c), 0)
    ci = lax.broadcasted_iota(jnp.int32, (c, c), 1)
    tri_incl = jnp.where(ci <= ri, 1.0, 0.0).astype(BF16)
    lane_h0 = lax.broadcasted_iota(jnp.int32, (c, LANES), 1) < RW_HEAD_DIM
    strict = lane < rowi
    incl = lane <= rowi
    eye = lane == rowi
    eye_f = jnp.where(eye, 1.0, 0.0)

    def bd(x):
        zero = jnp.zeros_like(x)
        return jnp.concatenate([jnp.where(lane_h0, x, zero), jnp.where(lane_h0, zero, x)], axis=0)

    def level_mask(m):
        return ((rowi // (2 * m)) == (lane // (2 * m))) & ((rowi % (2 * m)) >= m) & ((lane % (2 * m)) < m)

    level_masks = {m: level_mask(m) for m in (1, 2, 4, 8, 16, 32)}

    lw_parts = jnp.concatenate(_split_bf16(lw, 3), axis=1)
    cums = []
    for ic in range(nc):
        cum3 = _dot(tri_incl, lw_parts[ic * c:(ic + 1) * c, :])
        cums.append(cum3[:, :LANES] + cum3[:, LANES:2 * LANES] + cum3[:, 2 * LANES:])
    cum = jnp.concatenate(cums, axis=0)
    einv = jnp.exp(-cum)
    a_til = kk * jnp.exp(cum - lw)
    r_til = r * jnp.exp(cum)
    k_til = kmod * einv
    b_til = bb * einv

    chunks = range(nc)
    rows_of = lambda x, ic: x[ic * c:(ic + 1) * c]
    w_end = [jnp.exp(cums[ic][c - 1:c, :]) for ic in chunks]
    a_t = [bd(rows_of(a_til, ic).astype(BF16)) for ic in chunks]
    r_t = [bd(rows_of(r_til, ic)) for ic in chunks]
    k_t = [bd(rows_of(k_til, ic).astype(BF16)) for ic in chunks]
    b_t = [bd(rows_of(b_til, ic).astype(BF16)) for ic in chunks]
    k_e = [bd((rows_of(k_til, ic) * w_end[ic]).astype(BF16)) for ic in chunks]
    b_e = [bd((rows_of(b_til, ic) * w_end[ic]).astype(BF16)) for ic in chunks]
    v_b = [bd(rows_of(v, ic).astype(BF16)) for ic in chunks]

    sc = [_dot_nt(jnp.concatenate([a_t[ic], r_t[ic].astype(BF16)], axis=0),
                  jnp.concatenate([k_t[ic], b_t[ic]], axis=0)) for ic in chunks]
    l_ab = [jnp.where(strict, s_[:LANES, LANES:], 0.0) for s_ in sc]
    a_ak = [jnp.where(strict, s_[:LANES, :LANES], 0.0).astype(BF16) for s_ in sc]
    a_rk = [jnp.where(incl, s_[LANES:, :LANES], 0.0).astype(BF16) for s_ in sc]
    a_rb = [jnp.where(incl, s_[LANES:, LANES:], 0.0).astype(BF16) for s_ in sc]

    x = [eye_f - jnp.where(level_masks[1], l_, 0.0) for l_ in l_ab]
    for m in (2, 4, 8, 16, 32):
        xb = [x_.astype(BF16) for x_ in x]
        xc = [_dot(xb[ic], jnp.where(level_masks[m], l_ab[ic], 0.0).astype(BF16)) for ic in chunks]
        x = [x[ic] - _dot(xc[ic].astype(BF16), xb[ic]) for ic in chunks]

    akv = [_dot(a_ak[ic], v_b[ic]) for ic in chunks]
    rkv = [_dot(a_rk[ic], v_b[ic]) for ic in chunks]
    ktv = [_dot_tn(k_e[ic], v_b[ic]) for ic in chunks]
    ua = [_dot(x[ic].astype(BF16), jnp.concatenate([akv[ic].astype(BF16), a_t[ic]], axis=1)).astype(BF16)
          for ic in chunks]
    rb_ua = [_dot(a_rb[ic], ua[ic]) for ic in chunks]
    bt_ua = [_dot_tn(b_e[ic], ua[ic]) for ic in chunks]
    r_bars = [(r_t[ic] - rb_ua[ic][:, LANES:]).astype(BF16) for ic in chunks]
    y0 = [rkv[ic] - rb_ua[ic][:, :LANES] for ic in chunks]
    y0s = [y_[:c, :] + y_[c:, :] for y_ in y0]
    g_mats = [(jnp.where(eye, w_end[ic], 0.0) - bt_ua[ic][:, LANES:]).astype(BF16) for ic in chunks]
    h_mats = [ktv[ic] - bt_ua[ic][:, :LANES] for ic in chunks]

    pairs = range(nc // 2)
    gh2 = [_dot(g_mats[2 * k + 1], jnp.concatenate([g_mats[2 * k], h_mats[2 * k].astype(BF16)], axis=1))
           for k in pairs]
    g2 = [gh[:, :LANES].astype(BF16) for gh in gh2]
    h2 = [gh2[k][:, LANES:] + h_mats[2 * k + 1] for k in pairs]
    p = p_ref[...]
    p_even = []
    for k in pairs:
        pb = p.astype(BF16)
        p_even.append(pb)
        p = _dot(g2[k], pb) + h2[k]
    p_ref[...] = p
    p_odd = [(_dot(g_mats[2 * k], p_even[k]) + h_mats[2 * k]).astype(BF16) for k in pairs]
    states = [p_even[ic // 2] if ic % 2 == 0 else p_odd[ic // 2] for ic in chunks]
    y_bd = [_dot(r_bars[ic], states[ic]) for ic in chunks]
    ys = [y_bd[ic][:c, :] + y_bd[ic][c:, :] + y0s[ic] for ic in chunks]

    yv = jnp.concatenate(ys, axis=0)
    inv_d = 1.0 / RW_HEAD_DIM
    mean = head_sum(yv) * inv_d
    yc = yv - mean
    var = head_sum(yc * yc) * inv_d
    yn = yc * lax.rsqrt(var + RW_GN_EPS) * gng_ref[...] + gnb_ref[...]
    bonus = head_sum(r * kmod * rk_ref[...]) * v
    o_ref[0] = ((yn + bonus) * gate).astype(o_ref.dtype)


def _wkv(z_rw, mu, w0, w2, a0, a2, g2, k_k, k_a, r_k, gn_g, gn_b, ts):
    b, s, _ = z_rw.shape
    assert s % ts == 0 and ts % (2 * WKV_CHUNK) == 0
    npair = RW_WIDTH // LANES
    lo_blk = (3 * RW_WIDTH) // (2 * LANES)
    row = lambda a: a.reshape(1, -1)

    def col(off):
        return pl.BlockSpec((1, ts, LANES), lambda ib, ip, it: (ib, it, off + ip))

    def vec(off):
        return pl.BlockSpec((1, LANES), lambda ib, ip, it: (0, off + ip))

    in_specs = [
        col(0), col(npair), col(2 * npair),
        pl.BlockSpec((1, ts, 2 * LANES), lambda ib, ip, it: (ib, it, lo_blk)),
        vec(0), vec(npair), vec(2 * npair),
        pl.BlockSpec((1, 2 * LANES), lambda ib, ip, it: (0, lo_blk)),
        vec(0), vec(0), vec(0), vec(0), vec(0), vec(0), vec(0),
        pl.BlockSpec((RW_DECAY_LORA, LANES), lambda ib, ip, it: (0, ip)),
        pl.BlockSpec((RW_ICLR_LORA, LANES), lambda ib, ip, it: (0, ip)),
        pl.BlockSpec((RW_GATE_LORA, LANES), lambda ib, ip, it: (0, ip)),
    ]
    scratch = [
        pltpu.VMEM((LANES, LANES), F32),
        pltpu.VMEM((1, LANES), F32), pltpu.VMEM((1, LANES), F32), pltpu.VMEM((1, LANES), F32),
        pltpu.VMEM((1, 2 * LANES), F32),
    ]
    mu2 = row(mu)
    return pl.pallas_call(
        _wkv_kernel,
        grid=(b, npair, s // ts),
        in_specs=in_specs,
        out_specs=pl.BlockSpec((1, ts, LANES), lambda ib, ip, it: (ib, it, ip)),
        out_shape=jax.ShapeDtypeStruct((b, s, RW_WIDTH), BF16),
        scratch_shapes=scratch,
        compiler_params=_cparams(("parallel", "parallel", "arbitrary")),
        name="wkv",
    )(z_rw, z_rw, z_rw, z_rw, mu2, mu2, mu2, mu2,
      row(w0), row(a0), row(k_k), row(k_a), row(r_k), row(gn_g), row(gn_b),
      w2.astype(BF16), a2.astype(BF16), g2.astype(BF16))


ATTN_ROWS = 32


def _attn_kernel(lq1_ref, lk1_ref, lq2_ref, lk2_ref, q_ref, k_ref, v_ref, subg_ref, o_ref,
                 qs_ref, s_ref, p_ref, m_ref, alpha_ref, acc_ref, *, lambda_init, tk):
    tq = q_ref.shape[1]
    rows = 2 * tq
    qi = pl.program_id(2)
    ngroups = rows // ATTN_ROWS
    heads = (0, 1)

    lane_h0 = lax.broadcasted_iota(jnp.int32, (tq, LANES), 1) < DA_QK_DIM
    for h in heads:
        q = q_ref[0, :, h * LANES:(h + 1) * LANES]
        zero = jnp.zeros_like(q)
        qsc = (q.astype(F32) * (DA_QK_DIM ** -0.5 * math.log2(math.e))).astype(BF16)
        qs_ref[h] = jnp.concatenate([jnp.where(lane_h0, qsc, zero), jnp.where(lane_h0, zero, qsc)], axis=0)
    m_ref[...] = jnp.full_like(m_ref, -jnp.inf)
    alpha_ref[...] = jnp.zeros_like(alpha_ref)
    acc_ref[...] = jnp.zeros_like(acc_ref)
    p_ref[1] = jnp.zeros((rows, tk), BF16)

    key_chunk = lax.broadcasted_iota(jnp.int32, (ATTN_ROWS, LANES), 1) // CHUNK
    ones = jnp.ones((tk, LANES), BF16)

    def scores(h, j, width=tk):
        koff = pl.multiple_of(j * tk, tk)
        s_ref[h, :, :width] = _dot_nt(qs_ref[h], k_ref[0, pl.ds(koff, width), h * LANES:(h + 1) * LANES])

    def accumulate(h, j, width=tk):
        koff = pl.multiple_of(j * tk, tk)
        v_ext = jnp.concatenate([v_ref[0, pl.ds(koff, width), h * LANES:(h + 1) * LANES], ones[:width]], axis=1)
        alpha = alpha_ref[h]
        acc_ref[h] = jnp.concatenate([alpha, alpha], axis=1) * acc_ref[h] + _dot(p_ref[h, :, :width], v_ext)

    def softmax(h, j, width=tk, first_masked_tile=None):
        def load_scores(g, t):
            r0 = g * ATTN_ROWS
            s = s_ref[h, r0:r0 + ATTN_ROWS, t * LANES:(t + 1) * LANES]
            if first_masked_tile is not None and t >= first_masked_tile:
                q_chunk = (qi * tq + r0 % tq) // CHUNK - j * (tk // CHUNK) - t * (LANES // CHUNK)
                s = jnp.where(key_chunk <= q_chunk, s, -jnp.inf)
            return s

        tiles = range(width // LANES)
        for g in range(ngroups):
            sl = slice(g * ATTN_ROWS, (g + 1) * ATTN_ROWS)
            m_prev = m_ref[h, sl, :]
            tile_max = load_scores(g, 0)
            for t in tiles[1:]:
                tile_max = jnp.maximum(tile_max, load_scores(g, t))
            m_new = jnp.maximum(m_prev, jnp.max(tile_max, axis=-1, keepdims=True))
            alpha_ref[h, sl, :] = jnp.exp2(m_prev - m_new)
            m_ref[h, sl, :] = m_new
        for g in range(ngroups):
            sl = slice(g * ATTN_ROWS, (g + 1) * ATTN_ROWS)
            m_row = m_ref[h, sl, :]
            for t in tiles:
                p_ref[h, sl, t * LANES:(t + 1) * LANES] = jnp.exp2(load_scores(g, t) - m_row).astype(BF16)

    def step(j, width=tk, first_masked_tile=None):
        scores(1, j, width)
        accumulate(1, jnp.maximum(j - 1, 0))
        softmax(0, j, width, first_masked_tile)
        accumulate(0, j, width)
        if first_masked_tile is None:
            scores(0, j + 1)
        softmax(1, j, width, first_masked_tile)

    nfull = (qi * tq + CHUNK) // tk
    scores(0, 0)

    def full_step(j, carry):
        step(j)
        return carry

    lax.fori_loop(0, nfull, full_step, 0)
    q_per_k = tk // tq
    for v in range(q_per_k):
        @pl.when(qi % q_per_k == v)
        def _(v=v):
            width = (v + 1) * tq
            step(nfull, width, v * tq // LANES)
            accumulate(1, nfull, width)

    lam = (jnp.exp(jnp.sum(lq1_ref[...] * lk1_ref[...], axis=-1, keepdims=True))
           - jnp.exp(jnp.sum(lq2_ref[...] * lk2_ref[...], axis=-1, keepdims=True))
           + lambda_init)
    for h in heads:
        acc = acc_ref[h]
        o = acc[:, :DA_V_DIM] / acc[:, DA_V_DIM:]
        o = o[:tq, :] - lam * o[tq:, :]
        o = _rms(o, subg_ref[...], DA_SUBLN_EPS) * (1.0 - lambda_init)
        o_ref[0, :, h * LANES:(h + 1) * LANES] = o.astype(o_ref.dtype)


def _attn(z_da, lq1, lk1, lq2, lk2, sub_g, lambda_init, tq, tk):
    b, s, _ = z_da.shape
    assert tk % tq == 0 and s % tk == 0
    row = lambda a: a.reshape(1, -1)
    lam_spec = pl.BlockSpec((1, DA_QK_DIM), lambda ib, ih, iq: (0, 0))
    kern = functools.partial(_attn_kernel, lambda_init=lambda_init, tk=tk)
    rows = 2 * tq
    pair = 2 * LANES
    npairs = DA_HEADS // 2
    rep = pltpu.VMEM((2, rows, LANES), F32)
    return pl.pallas_call(
        kern,
        grid=(b, npairs, s // tq),
        in_specs=[
            lam_spec, lam_spec, lam_spec, lam_spec,
            pl.BlockSpec((1, tq, pair), lambda ib, ih, iq: (ib, iq, ih)),
            pl.BlockSpec((1, s, pair), lambda ib, ih, iq: (ib, 0, npairs + ih)),
            pl.BlockSpec((1, s, pair), lambda ib, ih, iq: (ib, 0, 2 * npairs + ih)),
            pl.BlockSpec((1, DA_V_DIM), lambda ib, ih, iq: (0, 0)),
        ],
        out_specs=pl.BlockSpec((1, tq, pair), lambda ib, ih, iq: (ib, iq, ih)),
        out_shape=jax.ShapeDtypeStruct((b, s, DA_V_WIDTH), BF16),
        scratch_shapes=[
            pltpu.VMEM((2, rows, LANES), BF16),
            pltpu.VMEM((2, rows, tk), F32),
            pltpu.VMEM((2, rows, tk), BF16),
            rep, rep,
            pltpu.VMEM((2, rows, 2 * DA_V_DIM), F32),
        ],
        compiler_params=_cparams(("parallel", "parallel", "arbitrary")),
        name="attn",
    )(row(lq1), row(lk1), row(lq2), row(lk2), z_da, z_da, z_da, row(sub_g))


def _mix_kernel(x_ref, yrw_ref, yda_ref, grw_ref, gda_ref, wrw_ref, wda_ref, wo_ref, g_ref, o_ref):
    y_rw = _dot(yrw_ref[...], wrw_ref[...])
    y_da = _dot(yda_ref[...], wda_ref[...])
    merged = _sigmoid(grw_ref[...]) * y_rw + _sigmoid(gda_ref[...]) * y_da
    mo = _dot(merged.astype(BF16), wo_ref[...])
    o_ref[...] = x_ref[...] + _rms(mo, g_ref[...], RMS_EPS)


def _mix(x2d, y_rw, y_da, z_g, w_rw, w_da, w_o, g, tm):
    t, d = x2d.shape
    full = lambda shape: pl.BlockSpec(shape, lambda i: (0, 0))
    return pl.pallas_call(
        _mix_kernel,
        grid=(t // tm,),
        in_specs=[
            pl.BlockSpec((tm, d), lambda i: (i, 0)),
            pl.BlockSpec((tm, RW_WIDTH), lambda i: (i, 0)),
            pl.BlockSpec((tm, DA_V_WIDTH), lambda i: (i, 0)),
            pl.BlockSpec((tm, d), lambda i: (i, 0)),
            pl.BlockSpec((tm, d), lambda i: (i, 1)),
            full(w_rw.shape), full(w_da.shape), full(w_o.shape), full((1, d)),
        ],
        out_specs=pl.BlockSpec((tm, d), lambda i: (i, 0)),
        out_shape=jax.ShapeDtypeStruct((t, d), F32),
        compiler_params=_cparams(("parallel",)),
        name="mix",
    )(x2d, y_rw, y_da, z_g, z_g, w_rw, w_da, w_o, g)


def _mlp_kernel(x_ref, gpre_ref, wup_ref, wdn_ref, gpost_ref, o_ref, h_ref, acc_ref):
    j = pl.program_id(1)

    @pl.when(j == 0)
    def _():
        h_ref[...] = _rms(x_ref[...], gpre_ref[...], RMS_EPS).astype(BF16)
        acc_ref[...] = jnp.zeros_like(acc_ref)

    up = _dot(h_ref[...], wup_ref[...])
    act = jnp.square(jnp.maximum(up, 0.0))
    acc_ref[...] += _dot(act.astype(BF16), wdn_ref[...])

    @pl.when(j == pl.num_programs(1) - 1)
    def _():
        o_ref[...] = x_ref[...] + _rms(acc_ref[...], gpost_ref[...], RMS_EPS)


def _mlp(x2d, g_pre, w_up, w_dn, g_post, tm, tf):
    t, d = x2d.shape
    ff = w_up.shape[1]
    return pl.pallas_call(
        _mlp_kernel,
        grid=(t // tm, ff // tf),
        in_specs=[
            pl.BlockSpec((tm, d), lambda i, j: (i, 0)),
            pl.BlockSpec((1, d), lambda i, j: (0, 0)),
            pl.BlockSpec((d, tf), lambda i, j: (0, j)),
            pl.BlockSpec((tf, d), lambda i, j: (j, 0)),
            pl.BlockSpec((1, d), lambda i, j: (0, 0)),
        ],
        out_specs=pl.BlockSpec((tm, d), lambda i, j: (i, 0)),
        out_shape=jax.ShapeDtypeStruct((t, d), F32),
        scratch_shapes=[pltpu.VMEM((tm, d), BF16), pltpu.VMEM((tm, d), F32)],
        compiler_params=_cparams(("parallel", "arbitrary")),
        name="mlp",
    )(x2d, g_pre, w_up, w_dn, g_post)


def _tile(n, want):
    t = min(n, want)
    while n % t:
        t //= 2
    return t


class _Tiles(NamedTuple):
    inproj_rows: int
    wkv_tokens: int
    attn_q: int
    attn_k: int
    mix_rows: int
    mlp_rows: int
    mlp_ff: int


def _plan(batch, seq, d_ff):
    tokens = batch * seq
    attn_q = _tile(seq, 512)
    return _Tiles(
        inproj_rows=_tile(tokens, 256),
        wkv_tokens=_tile(seq, 16 * WKV_CHUNK),
        attn_q=attn_q,
        attn_k=_tile(seq, 2 * attn_q),
        mix_rows=_tile(tokens, 512),
        mlp_rows=_tile(tokens, 1024),
        mlp_ff=_tile(d_ff, 1024),
    )


def kernel(x, pre_mix_g, w_in, rw_shift_mu, rw_w0, rw_w2, rw_a0, rw_a2, rw_g2, rw_k_k, rw_k_a, rw_r_k, rw_gn_g, rw_gn_b, w_rw_out, da_lam_q1, da_lam_k1, da_lam_q2, da_lam_k2, da_sub_g, w_da_out, w_o, post_mix_g, pre_mlp_g, w_up, w_down, post_mlp_g):
    b, s, d = x.shape
    depth = w_in.shape[0]
    t = b * s
    tiles = _plan(b, s, w_up.shape[2])
    for l in range(depth):
        lambda_init = 0.8 - 0.6 * math.exp(-0.3 * l)
        x2d = x.reshape(t, d)
        z_rw, z_da, z_g = _inproj(x2d, pre_mix_g[l].reshape(1, d), w_in[l].astype(BF16), tiles.inproj_rows)

        y_rw = _wkv(z_rw.reshape(b, s, RW_IN), rw_shift_mu[l], rw_w0[l], rw_w2[l], rw_a0[l],
                    rw_a2[l], rw_g2[l], rw_k_k[l], rw_k_a[l], rw_r_k[l], rw_gn_g[l], rw_gn_b[l],
                    tiles.wkv_tokens)
        y_da = _attn(z_da.reshape(b, s, DA_IN), da_lam_q1[l], da_lam_k1[l], da_lam_q2[l],
                     da_lam_k2[l], da_sub_g[l], lambda_init, tiles.attn_q, tiles.attn_k)

        x1 = _mix(x2d, y_rw.reshape(t, RW_WIDTH), y_da.reshape(t, DA_V_WIDTH), z_g,
                  w_rw_out[l].astype(BF16), w_da_out[l].astype(BF16), w_o[l].astype(BF16),
                  post_mix_g[l].reshape(1, d), tiles.mix_rows)
        x2 = _mlp(x1, pre_mlp_g[l].reshape(1, d), w_up[l].astype(BF16), w_down[l].astype(BF16),
                  post_mlp_g[l].reshape(1, d), tiles.mlp_rows, tiles.mlp_ff)
        x = x2.reshape(b, s, d)
    return x
```

```python
import functools
import math
from typing import NamedTuple

import jax
import jax.numpy as jnp
from jax import lax
from jax.experimental import pallas as pl
from jax.experimental.pallas import tpu as pltpu

F32 = jnp.float32
BF16 = jnp.bfloat16

RMS_EPS = 1e-6
CHUNK = 64
RW_HEADS = 8
RW_HEAD_DIM = 64
RW_WIDTH = RW_HEADS * RW_HEAD_DIM
RW_DECAY_LORA = 64
RW_ICLR_LORA = 64
RW_GATE_LORA = 128
RW_GN_EPS = 1e-5 * RW_HEAD_DIM
RW_IN = 3 * RW_WIDTH + RW_DECAY_LORA + RW_ICLR_LORA + RW_GATE_LORA
DA_HEADS = 4
DA_QK_DIM = 64
DA_V_DIM = 2 * DA_QK_DIM
DA_QK_WIDTH = DA_HEADS * 2 * DA_QK_DIM
DA_V_WIDTH = DA_HEADS * DA_V_DIM
DA_IN = 2 * DA_QK_WIDTH + DA_V_WIDTH
DA_SUBLN_EPS = 1e-5

LANES = 128
WKV_CHUNK = 64
VMEM_LIMIT = 56 * 1024 * 1024


def _cparams(semantics):
    return pltpu.CompilerParams(dimension_semantics=semantics, vmem_limit_bytes=VMEM_LIMIT)


def _rms(x, g, eps):
    return x * lax.rsqrt(jnp.mean(x * x, axis=-1, keepdims=True) + eps) * g


def _dot(a, b):
    return jnp.dot(a, b, preferred_element_type=F32)


def _dot_nt(a, b):
    return lax.dot_general(a, b, (((1,), (1,)), ((), ())), preferred_element_type=F32)


def _dot_tn(a, b):
    return lax.dot_general(a, b, (((0,), (0,)), ((), ())), preferred_element_type=F32)


def _inproj_kernel(x_ref, g_ref, w_ref, orw_ref, oda_ref, og_ref):
    h = _rms(x_ref[...], g_ref[...], RMS_EPS).astype(BF16)
    o1, o2 = RW_IN, RW_IN + DA_IN
    orw_ref[...] = _dot(h, w_ref[:, :o1])
    oda_ref[...] = _dot(h, w_ref[:, o1:o2]).astype(oda_ref.dtype)
    og_ref[...] = _sigmoid(_dot(h, w_ref[:, o2:])).astype(og_ref.dtype)


def _inproj(x2d, g, w_bf16, tm):
    t, d = x2d.shape
    n = w_bf16.shape[1]
    n_gate = n - RW_IN - DA_IN
    return pl.pallas_call(
        _inproj_kernel,
        grid=(t // tm,),
        in_specs=[
            pl.BlockSpec((tm, d), lambda i: (i, 0)),
            pl.BlockSpec((1, d), lambda i: (0, 0)),
            pl.BlockSpec((d, n), lambda i: (0, 0)),
        ],
        out_specs=[
            pl.BlockSpec((tm, RW_IN), lambda i: (i, 0)),
            pl.BlockSpec((tm, DA_IN), lambda i: (i, 0)),
            pl.BlockSpec((tm, n_gate), lambda i: (i, 0)),
        ],
        out_shape=[
            jax.ShapeDtypeStruct((t, RW_IN), F32),
            jax.ShapeDtypeStruct((t, DA_IN), BF16),
            jax.ShapeDtypeStruct((t, n_gate), BF16),
        ],
        compiler_params=_cparams(("parallel",)),
        name="inproj",
    )(x2d, g, w_bf16)


def _sigmoid(x):
    return 1.0 / (1.0 + jnp.exp(-x))


def _split_bf16(x, n):
    parts = []
    for _ in range(n - 1):
        hi = x.astype(BF16)
        parts.append(hi)
        x = x - hi.astype(F32)
    parts.append(x.astype(BF16))
    return parts


def _wkv_kernel(r_ref, k_ref, v_ref, lo_ref, mur_ref, muk_ref, muv_ref, mulo_ref,
                w0_ref, a0_ref, kk_ref, ka_ref, rk_ref, gng_ref, gnb_ref,
                w2_ref, a2_ref, g2_ref, o_ref,
                p_ref, cr_ref, ck_ref, cv_ref, clo_ref):
    ts = r_ref.shape[1]
    nc = ts // WKV_CHUNK
    c = WKV_CHUNK

    @pl.when(pl.program_id(2) == 0)
    def _():
        p_ref[...] = jnp.zeros_like(p_ref)
        cr_ref[...] = jnp.zeros_like(cr_ref)
        ck_ref[...] = jnp.zeros_like(ck_ref)
        cv_ref[...] = jnp.zeros_like(cv_ref)
        clo_ref[...] = jnp.zeros_like(clo_ref)

    def shifted(z, carry_ref, mu):
        prev = pltpu.roll(z, 1, axis=0)
        row = lax.broadcasted_iota(jnp.int32, z.shape, 0)
        prev = jnp.where(row == 0, carry_ref[...], prev)
        carry_ref[...] = z[ts - 1:ts, :]
        return z + (prev - z) * mu

    r = shifted(r_ref[0], cr_ref, mur_ref[...])
    k = shifted(k_ref[0], ck_ref, muk_ref[...])
    v = shifted(v_ref[0], cv_ref, muv_ref[...])
    lo = shifted(lo_ref[0], clo_ref, mulo_ref[...])
    xwa = lo[:, :LANES]
    xg = lo[:, LANES:]

    zeros64 = jnp.zeros((RW_DECAY_LORA, LANES), BF16)
    w2p = jnp.concatenate([w2_ref[...], zeros64], axis=0)
    a2p = jnp.concatenate([zeros64, a2_ref[...]], axis=0)
    wl = _dot(jnp.tanh(xwa).astype(BF16), w2p)
    al = _dot(xwa.astype(BF16), a2p)
    gate = _dot(_sigmoid(xg).astype(BF16), g2_ref[...])

    y = -(w0_ref[...] + wl)
    softplus = jnp.maximum(y, 0.0) + jnp.log(1.0 + jnp.exp(-jnp.abs(y)))
    lw = -jnp.exp(-softplus - 0.5)
    a = _sigmoid(a0_ref[...] + al)

    lane = lax.broadcasted_iota(jnp.int32, (LANES, LANES), 1)
    rowi = lax.broadcasted_iota(jnp.int32, (LANES, LANES), 0)
    same_head = (lane // RW_HEAD_DIM) == (rowi // RW_HEAD_DIM)
    head_ones = jnp.where(same_head, 1.0, 0.0).astype(BF16)

    def head_sum(x):
        hi, lo = _split_bf16(x, 2)
        return _dot(hi, head_ones) + _dot(lo, head_ones)

    kk = k * kk_ref[...]
    kk = kk / jnp.maximum(jnp.sqrt(head_sum(kk * kk)), 1e-12)
    kmod = k * (1.0 + (a - 1.0) * ka_ref[...])

    bb = kk * a

    ri = lax.broadcasted_iota(jnp.int32, (c, c), 0)
    ci = lax.broadcasted_iota(jnp.int32, (c, c), 1)
    tri_incl = jnp.where(ci <= ri, 1.0, 0.0).astype(BF16)
    lane_h0 = lax.broadcasted_iota(jnp.int32, (c, LANES), 1) < RW_HEAD_DIM
    strict = lane < rowi
    incl = lane <= rowi
    eye = lane == rowi
    eye_f = jnp.where(eye, 1.0, 0.0)

    def bd(x):
        zero = jnp.zeros_like(x)
        return jnp.concatenate([jnp.where(lane_h0, x, zero), jnp.where(lane_h0, zero, x)], axis=0)

    def level_mask(m):
        return ((rowi // (2 * m)) == (lane // (2 * m))) & ((rowi % (2 * m)) >= m) & ((lane % (2 * m)) < m)

    level_masks = {m: level_mask(m) for m in (1, 2, 4, 8, 16, 32)}

    lw_parts = jnp.concatenate(_split_bf16(lw, 3), axis=1)
    cums = []
    for ic in range(nc):
        cum3 = _dot(tri_incl, lw_parts[ic * c:(ic + 1) * c, :])
        cums.append(cum3[:, :LANES] + cum3[:, LANES:2 * LANES] + cum3[:, 2 * LANES:])
    cum = jnp.concatenate(cums, axis=0)
    einv = jnp.exp(-cum)
    a_til = kk * jnp.exp(cum - lw)
    r_til = r * jnp.exp(cum)
    k_til = kmod * einv
    b_til = bb * einv

    chunks = range(nc)
    rows_of = lambda x, ic: x[ic * c:(ic + 1) * c]
    w_end = [jnp.exp(cums[ic][c - 1:c, :]) for ic in chunks]
    a_t = [bd(rows_of(a_til, ic).astype(BF16)) for ic in chunks]
    r_t = [bd(rows_of(r_til, ic)) for ic in chunks]
    k_t = [bd(rows_of(k_til, ic).astype(BF16)) for ic in chunks]
    b_t = [bd(rows_of(b_til, ic).astype(BF16)) for ic in chunks]
    k_e = [bd((rows_of(k_til, ic) * w_end[ic]).astype(BF16)) for ic in chunks]
    b_e = [bd((rows_of(b_til, ic) * w_end[ic]).astype(BF16)) for ic in chunks]
    v_b = [bd(rows_of(v, ic).astype(BF16)) for ic in chunks]

    sc = [_dot_nt(jnp.concatenate([a_t[ic], r_t[ic].astype(BF16)], axis=0),
                  jnp.concatenate([k_t[ic], b_t[ic]], axis=0)) for ic in chunks]
    l_ab = [jnp.where(strict, s_[:LANES, LANES:], 0.0) for s_ in sc]
    a_ak = [jnp.where(strict, s_[:LANES, :LANES], 0.0).astype(BF16) for s_ in sc]
    a_rk = [jnp.where(incl, s_[LANES:, :LANES], 0.0).astype(BF16) for s_ in sc]
    a_rb = [jnp.where(incl, s_[LANES:, LANES:], 0.0).astype(BF16) for s_ in sc]

    x = [eye_f - jnp.where(level_masks[1], l_, 0.0) for l_ in l_ab]
    for m in (2, 4, 8, 16, 32):
        xb = [x_.astype(BF16) for x_ in x]
        xc = [_dot(xb[ic], jnp.where(level_masks[m], l_ab[ic], 0.0).astype(BF16)) for ic in chunks]
        x = [x[ic] - _dot(xc[ic].astype(BF16), xb[ic]) for ic in chunks]

    akv = [_dot(a_ak[ic], v_b[ic]) for ic in chunks]
    rkv = [_dot(a_rk[ic], v_b[ic]) for ic in chunks]
    ktv = [_dot_tn(k_e[ic], v_b[ic]) for ic in chunks]
    ua = [_dot(x[ic].astype(BF16), jnp.concatenate([akv[ic].astype(BF16), a_t[ic]], axis=1)).astype(BF16)
          for ic in chunks]
    rb_ua = [_dot(a_rb[ic], ua[ic]) for ic in chunks]
    bt_ua = [_dot_tn(b_e[ic], ua[ic]) for ic in chunks]
    r_bars = [(r_t[ic] - rb_ua[ic][:, LANES:]).astype(BF16) for ic in chunks]
    y0 = [rkv[ic] - rb_ua[ic][:, :LANES] for ic in chunks]
    y0s = [y_[:c, :] + y_[c:, :] for y_ in y0]
    g_mats = [(jnp.where(eye, w_end[ic], 0.0) - bt_ua[ic][:, LANES:]).astype(BF16) for ic in chunks]
    h_mats = [ktv[ic] - bt_ua[ic][:, :LANES] for ic in chunks]

    pairs = range(nc // 2)
    gh2 = [_dot(g_mats[2 * k + 1], jnp.concatenate([g_mats[2 * k], h_mats[2 * k].astype(BF16)], axis=1))
           for k in pairs]
    g2 = [gh[:, :LANES].astype(BF16) for gh in gh2]
    h2 = [gh2[k][:, LANES:] + h_mats[2 * k + 1] for k in pairs]
    p = p_ref[...]
    p_even = []
    for k in pairs:
        pb = p.astype(BF16)
        p_even.append(pb)
        p = _dot(g2[k], pb) + h2[k]
    p_ref[...] = p
    p_odd = [(_dot(g_mats[2 * k], p_even[k]) + h_mats[2 * k]).astype(BF16) for k in pairs]
    states = [p_even[ic // 2] if ic % 2 == 0 else p_odd[ic // 2] for ic in chunks]
    y_bd = [_dot(r_bars[ic], states[ic]) for ic in chunks]
    ys = [y_bd[ic][:c, :] + y_bd[ic][c:, :] + y0s[ic] for ic in chunks]

    yv = jnp.concatenate(ys, axis=0)
    inv_d = 1.0 / RW_HEAD_DIM
    mean = head_sum(yv) * inv_d
    yc = yv - mean
    var = head_sum(yc * yc) * inv_d
    yn = yc * lax.rsqrt(var + RW_GN_EPS) * gng_ref[...] + gnb_ref[...]
    bonus = head_sum(r * kmod * rk_ref[...]) * v
    o_ref[0] = ((yn + bonus) * gate).astype(o_ref.dtype)


def _wkv(z_rw, mu, w0, w2, a0, a2, g2, k_k, k_a, r_k, gn_g, gn_b, ts):
    b, s, _ = z_rw.shape
    assert s % ts == 0 and ts % (2 * WKV_CHUNK) == 0
    npair = RW_WIDTH // LANES
    lo_blk = (3 * RW_WIDTH) // (2 * LANES)
    row = lambda a: a.reshape(1, -1)

    def col(off):
        return pl.BlockSpec((1, ts, LANES), lambda ib, ip, it: (ib, it, off + ip))

    def vec(off):
        return pl.BlockSpec((1, LANES), lambda ib, ip, it: (0, off + ip))

    in_specs = [
        col(0), col(npair), col(2 * npair),
        pl.BlockSpec((1, ts, 2 * LANES), lambda ib, ip, it: (ib, it, lo_blk)),
        vec(0), vec(npair), vec(2 * npair),
        pl.BlockSpec((1, 2 * LANES), lambda ib, ip, it: (0, lo_blk)),
        vec(0), vec(0), vec(0), vec(0), vec(0), vec(0), vec(0),
        pl.BlockSpec((RW_DECAY_LORA, LANES), lambda ib, ip, it: (0, ip)),
        pl.BlockSpec((RW_ICLR_LORA, LANES), lambda ib, ip, it: (0, ip)),
        pl.BlockSpec((RW_GATE_LORA, LANES), lambda ib, ip, it: (0, ip)),
    ]
    scratch = [
        pltpu.VMEM((LANES, LANES), F32),
        pltpu.VMEM((1, LANES), F32), pltpu.VMEM((1, LANES), F32), pltpu.VMEM((1, LANES), F32),
        pltpu.VMEM((1, 2 * LANES), F32),
    ]
    mu2 = row(mu)
    return pl.pallas_call(
        _wkv_kernel,
        grid=(b, npair, s // ts),
        in_specs=in_specs,
        out_specs=pl.BlockSpec((1, ts, LANES), lambda ib, ip, it: (ib, it, ip)),
        out_shape=jax.ShapeDtypeStruct((b, s, RW_WIDTH), BF16),
        scratch_shapes=scratch,
        compiler_params=_cparams(("parallel", "parallel", "arbitrary")),
        name="wkv",
    )(z_rw, z_rw, z_rw, z_rw, mu2, mu2, mu2, mu2,
      row(w0), row(a0), row(k_k), row(k_a), row(r_k), row(gn_g), row(gn_b),
      w2.astype(BF16), a2.astype(BF16), g2.astype(BF16))


ATTN_ROWS = 32
assert CHUNK % ATTN_ROWS == 0


def _attn_kernel(lq1_ref, lk1_ref, lq2_ref, lk2_ref, q_ref, k_ref, v_ref, subg_ref, o_ref,
                 qs_ref, s_ref, p_ref, m_ref, alpha_ref, acc_ref, *, lambda_init, tk):
    tq = q_ref.shape[1]
    rows = 2 * tq
    qi = pl.program_id(2)
    ngroups = rows // ATTN_ROWS
    heads = (0, 1)

    lane_h0 = lax.broadcasted_iota(jnp.int32, (tq, LANES), 1) < DA_QK_DIM
    for h in heads:
        q = q_ref[0, :, h * LANES:(h + 1) * LANES]
        zero = jnp.zeros_like(q)
        qsc = (q.astype(F32) * (DA_QK_DIM ** -0.5 * math.log2(math.e))).astype(BF16)
        qs_ref[h] = jnp.concatenate([jnp.where(lane_h0, qsc, zero), jnp.where(lane_h0, zero, qsc)], axis=0)
    m_ref[...] = jnp.full_like(m_ref, -jnp.inf)
    alpha_ref[...] = jnp.zeros_like(alpha_ref)
    acc_ref[...] = jnp.zeros_like(acc_ref)
    p_ref[1] = jnp.zeros((rows, tk), BF16)

    key_chunk = lax.broadcasted_iota(jnp.int32, (ATTN_ROWS, LANES), 1) // CHUNK
    ones = jnp.ones((tk, LANES), BF16)

    def scores(h, j, width=tk):
        koff = pl.multiple_of(j * tk, tk)
        s_ref[h, :, :width] = _dot_nt(qs_ref[h], k_ref[0, pl.ds(koff, width), h * LANES:(h + 1) * LANES])

    def accumulate(h, j, width=tk):
        koff = pl.multiple_of(j * tk, tk)
        v_ext = jnp.concatenate([v_ref[0, pl.ds(koff, width), h * LANES:(h + 1) * LANES], ones[:width]], axis=1)
        alpha = alpha_ref[h]
        acc_ref[h] = jnp.concatenate([alpha, alpha], axis=1) * acc_ref[h] + _dot(p_ref[h, :, :width], v_ext)

    def softmax(h, width=tk, q_offset=None):
        def load_scores(g, t):
            r0 = g * ATTN_ROWS
            s = s_ref[h, r0:r0 + ATTN_ROWS, t * LANES:(t + 1) * LANES]
            if q_offset is None:
                return s
            q_chunk = (q_offset + r0 % tq) // CHUNK
            first_key_chunk = t * (LANES // CHUNK)
            if first_key_chunk + LANES // CHUNK - 1 <= q_chunk:
                return s
            if first_key_chunk > q_chunk:
                return None
            return jnp.where(key_chunk <= q_chunk - first_key_chunk, s, -jnp.inf)

        tiles = range(width // LANES)
        for g in range(ngroups):
            sl = slice(g * ATTN_ROWS, (g + 1) * ATTN_ROWS)
            m_prev = m_ref[h, sl, :]
            visible = [s for s in (load_scores(g, t) for t in tiles) if s is not None]
            tile_max = visible[0]
            for s in visible[1:]:
                tile_max = jnp.maximum(tile_max, s)
            m_new = jnp.maximum(m_prev, jnp.max(tile_max, axis=-1, keepdims=True))
            alpha_ref[h, sl, :] = jnp.exp2(m_prev - m_new)
            m_ref[h, sl, :] = m_new
        for g in range(ngroups):
            sl = slice(g * ATTN_ROWS, (g + 1) * ATTN_ROWS)
            m_row = m_ref[h, sl, :]
            for t in tiles:
                s = load_scores(g, t)
                p = jnp.zeros((ATTN_ROWS, LANES), BF16) if s is None else jnp.exp2(s - m_row).astype(BF16)
                p_ref[h, sl, t * LANES:(t + 1) * LANES] = p

    def step(j, width=tk, q_offset=None):
        scores(1, j, width)
        accumulate(1, jnp.maximum(j - 1, 0))
        softmax(0, width, q_offset)
        accumulate(0, j, width)
        if q_offset is None:
            scores(0, j + 1)
        softmax(1, width, q_offset)

    nfull = (qi * tq + CHUNK) // tk
    scores(0, 0)

    def full_step(j, carry):
        step(j)
        return carry

    lax.fori_loop(0, nfull, full_step, 0)
    q_per_k = tk // tq
    for v in range(q_per_k):
        @pl.when(qi % q_per_k == v)
        def _(v=v):
            width = (v + 1) * tq
            step(nfull, width, v * tq)
            accumulate(1, nfull, width)

    lam = (jnp.exp(jnp.sum(lq1_ref[...] * lk1_ref[...], axis=-1, keepdims=True))
           - jnp.exp(jnp.sum(lq2_ref[...] * lk2_ref[...], axis=-1, keepdims=True))
           + lambda_init)
    for h in heads:
        acc = acc_ref[h]
        o = acc[:, :DA_V_DIM] / acc[:, DA_V_DIM:]
        o = o[:tq, :] - lam * o[tq:, :]
        o = _rms(o, subg_ref[...], DA_SUBLN_EPS) * (1.0 - lambda_init)
        o_ref[0, :, h * LANES:(h + 1) * LANES] = o.astype(o_ref.dtype)


def _attn(z_da, lq1, lk1, lq2, lk2, sub_g, lambda_init, tq, tk):
    b, s, _ = z_da.shape
    assert tk % tq == 0 and s % tk == 0
    row = lambda a: a.reshape(1, -1)
    lam_spec = pl.BlockSpec((1, DA_QK_DIM), lambda ib, ih, iq: (0, 0))
    kern = functools.partial(_attn_kernel, lambda_init=lambda_init, tk=tk)
    rows = 2 * tq
    pair = 2 * LANES
    npairs = DA_HEADS // 2
    rep = pltpu.VMEM((2, rows, LANES), F32)
    return pl.pallas_call(
        kern,
        grid=(b, npairs, s // tq),
        in_specs=[
            lam_spec, lam_spec, lam_spec, lam_spec,
            pl.BlockSpec((1, tq, pair), lambda ib, ih, iq: (ib, iq, ih)),
            pl.BlockSpec((1, s, pair), lambda ib, ih, iq: (ib, 0, npairs + ih)),
            pl.BlockSpec((1, s, pair), lambda ib, ih, iq: (ib, 0, 2 * npairs + ih)),
            pl.BlockSpec((1, DA_V_DIM), lambda ib, ih, iq: (0, 0)),
        ],
        out_specs=pl.BlockSpec((1, tq, pair), lambda ib, ih, iq: (ib, iq, ih)),
        out_shape=jax.ShapeDtypeStruct((b, s, DA_V_WIDTH), BF16),
        scratch_shapes=[
            pltpu.VMEM((2, rows, LANES), BF16),
            pltpu.VMEM((2, rows, tk), F32),
            pltpu.VMEM((2, rows, tk), BF16),
            rep, rep,
            pltpu.VMEM((2, rows, 2 * DA_V_DIM), F32),
        ],
        compiler_params=_cparams(("parallel", "parallel", "arbitrary")),
        name="attn",
    )(row(lq1), row(lk1), row(lq2), row(lk2), z_da, z_da, z_da, row(sub_g))


def _mix_kernel(x_ref, yrw_ref, yda_ref, grw_ref, gda_ref, wrw_ref, wda_ref, wo_ref, g_ref, o_ref):
    y_rw = _dot(yrw_ref[...], wrw_ref[...])
    y_da = _dot(yda_ref[...], wda_ref[...])
    merged = grw_ref[...].astype(F32) * y_rw + gda_ref[...].astype(F32) * y_da
    mo = _dot(merged.astype(BF16), wo_ref[...])
    o_ref[...] = x_ref[...] + _rms(mo, g_ref[...], RMS_EPS)


def _mix(x2d, y_rw, y_da, z_g, w_rw, w_da, w_o, g, tm):
    t, d = x2d.shape
    full = lambda shape: pl.BlockSpec(shape, lambda i: (0, 0))
    return pl.pallas_call(
        _mix_kernel,
        grid=(t // tm,),
        in_specs=[
            pl.BlockSpec((tm, d), lambda i: (i, 0)),
            pl.BlockSpec((tm, RW_WIDTH), lambda i: (i, 0)),
            pl.BlockSpec((tm, DA_V_WIDTH), lambda i: (i, 0)),
            pl.BlockSpec((tm, d), lambda i: (i, 0)),
            pl.BlockSpec((tm, d), lambda i: (i, 1)),
            full(w_rw.shape), full(w_da.shape), full(w_o.shape), full((1, d)),
        ],
        out_specs=pl.BlockSpec((tm, d), lambda i: (i, 0)),
        out_shape=jax.ShapeDtypeStruct((t, d), F32),
        compiler_params=_cparams(("parallel",)),
        name="mix",
    )(x2d, y_rw, y_da, z_g, z_g, w_rw, w_da, w_o, g)


def _mlp_kernel(x_ref, gpre_ref, wup_ref, wdn_ref, gpost_ref, o_ref, h_ref, acc_ref):
    j = pl.program_id(1)

    @pl.when(j == 0)
    def _():
        h_ref[...] = _rms(x_ref[...], gpre_ref[...], RMS_EPS).astype(BF16)
        acc_ref[...] = jnp.zeros_like(acc_ref)

    up = _dot(h_ref[...], wup_ref[...])
    act = jnp.square(jnp.maximum(up, 0.0))
    acc_ref[...] += _dot(act.astype(BF16), wdn_ref[...])

    @pl.when(j == pl.num_programs(1) - 1)
    def _():
        o_ref[...] = x_ref[...] + _rms(acc_ref[...], gpost_ref[...], RMS_EPS)


def _mlp(x2d, g_pre, w_up, w_dn, g_post, tm, tf):
    t, d = x2d.shape
    ff = w_up.shape[1]
    return pl.pallas_call(
        _mlp_kernel,
        grid=(t // tm, ff // tf),
        in_specs=[
            pl.BlockSpec((tm, d), lambda i, j: (i, 0)),
            pl.BlockSpec((1, d), lambda i, j: (0, 0)),
            pl.BlockSpec((d, tf), lambda i, j: (0, j)),
            pl.BlockSpec((tf, d), lambda i, j: (j, 0)),
            pl.BlockSpec((1, d), lambda i, j: (0, 0)),
        ],
        out_specs=pl.BlockSpec((tm, d), lambda i, j: (i, 0)),
        out_shape=jax.ShapeDtypeStruct((t, d), F32),
        scratch_shapes=[pltpu.VMEM((tm, d), BF16), pltpu.VMEM((tm, d), F32)],
        compiler_params=_cparams(("parallel", "arbitrary")),
        name="mlp",
    )(x2d, g_pre, w_up, w_dn, g_post)


def _tile(n, want):
    t = min(n, want)
    while n % t:
        t //= 2
    return t


class _Tiles(NamedTuple):
    inproj_rows: int
    wkv_tokens: int
    attn_q: int
    attn_k: int
    mix_rows: int
    mlp_rows: int
    mlp_ff: int


def _plan(batch, seq, d_ff):
    tokens = batch * seq
    attn_q = _tile(seq, 512)
    return _Tiles(
        inproj_rows=_tile(tokens, 256),
        wkv_tokens=_tile(seq, 16 * WKV_CHUNK),
        attn_q=attn_q,
        attn_k=_tile(seq, 2 * attn_q),
        mix_rows=_tile(tokens, 512),
        mlp_rows=_tile(tokens, 1024),
        mlp_ff=_tile(d_ff, 1024),
    )


def kernel(x, pre_mix_g, w_in, rw_shift_mu, rw_w0, rw_w2, rw_a0, rw_a2, rw_g2, rw_k_k, rw_k_a, rw_r_k, rw_gn_g, rw_gn_b, w_rw_out, da_lam_q1, da_lam_k1, da_lam_q2, da_lam_k2, da_sub_g, w_da_out, w_o, post_mix_g, pre_mlp_g, w_up, w_down, post_mlp_g):
    b, s, d = x.shape
    depth = w_in.shape[0]
    t = b * s
    tiles = _plan(b, s, w_up.shape[2])
    for l in range(depth):
        lambda_init = 0.8 - 0.6 * math.exp(-0.3 * l)
        x2d = x.reshape(t, d)
        z_rw, z_da, z_g = _inproj(x2d, pre_mix_g[l].reshape(1, d), w_in[l].astype(BF16), tiles.inproj_rows)

        y_rw = _wkv(z_rw.reshape(b, s, RW_IN), rw_shift_mu[l], rw_w0[l], rw_w2[l], rw_a0[l],
                    rw_a2[l], rw_g2[l], rw_k_k[l], rw_k_a[l], rw_r_k[l], rw_gn_g[l], rw_gn_b[l],
                    tiles.wkv_tokens)
        y_da = _attn(z_da.reshape(b, s, DA_IN), da_lam_q1[l], da_lam_k1[l], da_lam_q2[l],
                     da_lam_k2[l], da_sub_g[l], lambda_init, tiles.attn_q, tiles.attn_k)

        x1 = _mix(x2d, y_rw.reshape(t, RW_WIDTH), y_da.reshape(t, DA_V_WIDTH), z_g,
                  w_rw_out[l].astype(BF16), w_da_out[l].astype(BF16), w_o[l].astype(BF16),
                  post_mix_g[l].reshape(1, d), tiles.mix_rows)
        x2 = _mlp(x1, pre_mlp_g[l].reshape(1, d), w_up[l].astype(BF16), w_down[l].astype(BF16),
                  post_mlp_g[l].reshape(1, d), tiles.mlp_rows, tiles.mlp_ff)
        x = x2.reshape(b, s, d)
    return x
```

```python
import functools
import math
from typing import NamedTuple

import jax
import jax.numpy as jnp
from jax import lax
from jax.experimental import pallas as pl
from jax.experimental.pallas import tpu as pltpu

F32 = jnp.float32
BF16 = jnp.bfloat16

RMS_EPS = 1e-6
CHUNK = 64
RW_HEADS = 8
RW_HEAD_DIM = 64
RW_WIDTH = RW_HEADS * RW_HEAD_DIM
RW_DECAY_LORA = 64
RW_ICLR_LORA = 64
RW_GATE_LORA = 128
RW_GN_EPS = 1e-5 * RW_HEAD_DIM
RW_IN = 3 * RW_WIDTH + RW_DECAY_LORA + RW_ICLR_LORA + RW_GATE_LORA
DA_HEADS = 4
DA_QK_DIM = 64
DA_V_DIM = 2 * DA_QK_DIM
DA_QK_WIDTH = DA_HEADS * 2 * DA_QK_DIM
DA_V_WIDTH = DA_HEADS * DA_V_DIM
DA_IN = 2 * DA_QK_WIDTH + DA_V_WIDTH
DA_SUBLN_EPS = 1e-5

LANES = 128
WKV_CHUNK = 64
VMEM_LIMIT = 56 * 1024 * 1024


def _cparams(semantics):
    return pltpu.CompilerParams(dimension_semantics=semantics, vmem_limit_bytes=VMEM_LIMIT)


def _rms(x, g, eps):
    return x * lax.rsqrt(jnp.mean(x * x, axis=-1, keepdims=True) + eps) * g


def _dot(a, b):
    return jnp.dot(a, b, preferred_element_type=F32)


def _dot_nt(a, b):
    return lax.dot_general(a, b, (((1,), (1,)), ((), ())), preferred_element_type=F32)


def _dot_tn(a, b):
    return lax.dot_general(a, b, (((0,), (0,)), ((), ())), preferred_element_type=F32)


def _inproj_kernel(x_ref, g_ref, w_ref, orw_ref, oda_ref, og_ref):
    h = _rms(x_ref[...], g_ref[...], RMS_EPS).astype(BF16)
    o1, o2 = RW_IN, RW_IN + DA_IN
    orw_ref[...] = _dot(h, w_ref[:, :o1])
    oda_ref[...] = _dot(h, w_ref[:, o1:o2]).astype(oda_ref.dtype)
    og_ref[...] = _sigmoid(_dot(h, w_ref[:, o2:])).astype(og_ref.dtype)


def _inproj(x2d, g, w_bf16, tm):
    t, d = x2d.shape
    n = w_bf16.shape[1]
    n_gate = n - RW_IN - DA_IN
    return pl.pallas_call(
        _inproj_kernel,
        grid=(t // tm,),
        in_specs=[
            pl.BlockSpec((tm, d), lambda i: (i, 0)),
            pl.BlockSpec((1, d), lambda i: (0, 0)),
            pl.BlockSpec((d, n), lambda i: (0, 0)),
        ],
        out_specs=[
            pl.BlockSpec((tm, RW_IN), lambda i: (i, 0)),
            pl.BlockSpec((tm, DA_IN), lambda i: (i, 0)),
            pl.BlockSpec((tm, n_gate), lambda i: (i, 0)),
        ],
        out_shape=[
            jax.ShapeDtypeStruct((t, RW_IN), F32),
            jax.ShapeDtypeStruct((t, DA_IN), BF16),
            jax.ShapeDtypeStruct((t, n_gate), BF16),
        ],
        compiler_params=_cparams(("parallel",)),
        name="inproj",
    )(x2d, g, w_bf16)


def _sigmoid(x):
    return 1.0 / (1.0 + jnp.exp(-x))


def _split_bf16(x, n):
    parts = []
    for _ in range(n - 1):
        hi = x.astype(BF16)
        parts.append(hi)
        x = x - hi.astype(F32)
    parts.append(x.astype(BF16))
    return parts


def _wkv_kernel(r_ref, k_ref, v_ref, lo_ref, mur_ref, muk_ref, muv_ref, mulo_ref,
                w0_ref, a0_ref, kk_ref, ka_ref, rk_ref, gng_ref, gnb_ref,
                w2_ref, a2_ref, g2_ref, o_ref,
                p_ref, cr_ref, ck_ref, cv_ref, clo_ref):
    ts = r_ref.shape[1]
    nc = ts // WKV_CHUNK
    c = WKV_CHUNK

    @pl.when(pl.program_id(2) == 0)
    def _():
        p_ref[...] = jnp.zeros_like(p_ref)
        cr_ref[...] = jnp.zeros_like(cr_ref)
        ck_ref[...] = jnp.zeros_like(ck_ref)
        cv_ref[...] = jnp.zeros_like(cv_ref)
        clo_ref[...] = jnp.zeros_like(clo_ref)

    def shifted(z, carry_ref, mu):
        prev = pltpu.roll(z, 1, axis=0)
        row = lax.broadcasted_iota(jnp.int32, z.shape, 0)
        prev = jnp.where(row == 0, carry_ref[...], prev)
        carry_ref[...] = z[ts - 1:ts, :]
        return z + (prev - z) * mu

    r = shifted(r_ref[0], cr_ref, mur_ref[...])
    k = shifted(k_ref[0], ck_ref, muk_ref[...])
    v = shifted(v_ref[0], cv_ref, muv_ref[...])
    lo = shifted(lo_ref[0], clo_ref, mulo_ref[...])
    xwa = lo[:, :LANES]
    xg = lo[:, LANES:]

    zeros64 = jnp.zeros((RW_DECAY_LORA, LANES), BF16)
    w2p = jnp.concatenate([w2_ref[...], zeros64], axis=0)
    a2p = jnp.concatenate([zeros64, a2_ref[...]], axis=0)
    wl = _dot(jnp.tanh(xwa).astype(BF16), w2p)
    al = _dot(xwa.astype(BF16), a2p)
    gate = _dot(_sigmoid(xg).astype(BF16), g2_ref[...])

    y = -(w0_ref[...] + wl)
    softplus = jnp.maximum(y, 0.0) + jnp.log(1.0 + jnp.exp(-jnp.abs(y)))
    lw = -jnp.exp(-softplus - 0.5)
    a = _sigmoid(a0_ref[...] + al)

    lane = lax.broadcasted_iota(jnp.int32, (LANES, LANES), 1)
    rowi = lax.broadcasted_iota(jnp.int32, (LANES, LANES), 0)
    same_head = (lane // RW_HEAD_DIM) == (rowi // RW_HEAD_DIM)
    head_ones = jnp.where(same_head, 1.0, 0.0).astype(BF16)

    def head_sum(x):
        hi, lo = _split_bf16(x, 2)
        return _dot(hi, head_ones) + _dot(lo, head_ones)

    kk = k * kk_ref[...]
    kk = kk / jnp.maximum(jnp.sqrt(head_sum(kk * kk)), 1e-12)
    kmod = k * (1.0 + (a - 1.0) * ka_ref[...])

    bb = kk * a

    ri = lax.broadcasted_iota(jnp.int32, (c, c), 0)
    ci = lax.broadcasted_iota(jnp.int32, (c, c), 1)
    tri_incl = jnp.where(ci <= ri, 1.0, 0.0).astype(BF16)
    lane_h0 = lax.broadcasted_iota(jnp.int32, (c, LANES), 1) < RW_HEAD_DIM
    strict = lane < rowi
    incl = lane <= rowi
    eye = lane == rowi
    eye_f = jnp.where(eye, 1.0, 0.0)

    def bd(x):
        zero = jnp.zeros_like(x)
        return jnp.concatenate([jnp.where(lane_h0, x, zero), jnp.where(lane_h0, zero, x)], axis=0)

    def level_mask(m):
        return ((rowi // (2 * m)) == (lane // (2 * m))) & ((rowi % (2 * m)) >= m) & ((lane % (2 * m)) < m)

    level_masks = {m: level_mask(m) for m in (1, 2, 4, 8, 16, 32)}

    lw_parts = jnp.concatenate(_split_bf16(lw, 3), axis=1)
    cums = []
    for ic in range(nc):
        cum3 = _dot(tri_incl, lw_parts[ic * c:(ic + 1) * c, :])
        cums.append(cum3[:, :LANES] + cum3[:, LANES:2 * LANES] + cum3[:, 2 * LANES:])
    cum = jnp.concatenate(cums, axis=0)
    einv = jnp.exp(-cum)
    a_til = kk * jnp.exp(cum - lw)
    r_til = r * jnp.exp(cum)
    k_til = kmod * einv
    b_til = bb * einv

    chunks = range(nc)
    rows_of = lambda x, ic: x[ic * c:(ic + 1) * c]
    w_end = [jnp.exp(cums[ic][c - 1:c, :]) for ic in chunks]
    a_t = [bd(rows_of(a_til, ic).astype(BF16)) for ic in chunks]
    r_t = [bd(rows_of(r_til, ic)) for ic in chunks]
    k_t = [bd(rows_of(k_til, ic).astype(BF16)) for ic in chunks]
    b_t = [bd(rows_of(b_til, ic).astype(BF16)) for ic in chunks]
    k_e = [bd((rows_of(k_til, ic) * w_end[ic]).astype(BF16)) for ic in chunks]
    b_e = [bd((rows_of(b_til, ic) * w_end[ic]).astype(BF16)) for ic in chunks]
    v_b = [bd(rows_of(v, ic).astype(BF16)) for ic in chunks]

    sc = [_dot_nt(jnp.concatenate([a_t[ic], r_t[ic].astype(BF16)], axis=0),
                  jnp.concatenate([k_t[ic], b_t[ic]], axis=0)) for ic in chunks]
    l_ab = [jnp.where(strict, s_[:LANES, LANES:], 0.0) for s_ in sc]
    a_ak = [jnp.where(strict, s_[:LANES, :LANES], 0.0).astype(BF16) for s_ in sc]
    a_rk = [jnp.where(incl, s_[LANES:, :LANES], 0.0).astype(BF16) for s_ in sc]
    a_rb = [jnp.where(incl, s_[LANES:, LANES:], 0.0).astype(BF16) for s_ in sc]

    x = [eye_f - jnp.where(level_masks[1], l_, 0.0) for l_ in l_ab]
    for m in (2, 4, 8, 16, 32):
        xb = [x_.astype(BF16) for x_ in x]
        xc = [_dot(xb[ic], jnp.where(level_masks[m], l_ab[ic], 0.0).astype(BF16)) for ic in chunks]
        x = [x[ic] - _dot(xc[ic].astype(BF16), xb[ic]) for ic in chunks]

    akv = [_dot(a_ak[ic], v_b[ic]) for ic in chunks]
    rkv = [_dot(a_rk[ic], v_b[ic]) for ic in chunks]
    ktv = [_dot_tn(k_e[ic], v_b[ic]) for ic in chunks]
    ua = [_dot(x[ic].astype(BF16), jnp.concatenate([akv[ic].astype(BF16), a_t[ic]], axis=1)).astype(BF16)
          for ic in chunks]
    rb_ua = [_dot(a_rb[ic], ua[ic]) for ic in chunks]
    bt_ua = [_dot_tn(b_e[ic], ua[ic]) for ic in chunks]
    r_bars = [(r_t[ic] - rb_ua[ic][:, LANES:]).astype(BF16) for ic in chunks]
    y0 = [rkv[ic] - rb_ua[ic][:, :LANES] for ic in chunks]
    y0s = [y_[:c, :] + y_[c:, :] for y_ in y0]
    g_mats = [(jnp.where(eye, w_end[ic], 0.0) - bt_ua[ic][:, LANES:]).astype(BF16) for ic in chunks]
    h_mats = [ktv[ic] - bt_ua[ic][:, :LANES] for ic in chunks]

    def compose(g_lo, h_lo, g_hi, h_hi):
        gh = _dot(g_hi, jnp.concatenate([g_lo, h_lo.astype(BF16)], axis=1))
        return gh[:, :LANES].astype(BF16), gh[:, LANES:] + h_hi

    pairs = range(nc // 2)
    quads = range(nc // 4)
    gh2 = [compose(g_mats[2 * k], h_mats[2 * k], g_mats[2 * k + 1], h_mats[2 * k + 1]) for k in pairs]
    gh4 = [compose(*gh2[2 * q], *gh2[2 * q + 1]) for q in quads]
    p = p_ref[...]
    p_quad = []
    for q in quads:
        pb = p.astype(BF16)
        p_quad.append(pb)
        p = _dot(gh4[q][0], pb) + gh4[q][1]
    p_ref[...] = p
    p_mid = [(_dot(gh2[2 * q][0], p_quad[q]) + gh2[2 * q][1]).astype(BF16) for q in quads]
    p_even = [p_quad[k // 2] if k % 2 == 0 else p_mid[k // 2] for k in pairs]
    p_odd = [(_dot(g_mats[2 * k], p_even[k]) + h_mats[2 * k]).astype(BF16) for k in pairs]
    states = [p_even[ic // 2] if ic % 2 == 0 else p_odd[ic // 2] for ic in chunks]
    y_bd = [_dot(r_bars[ic], states[ic]) for ic in chunks]
    ys = [y_bd[ic][:c, :] + y_bd[ic][c:, :] + y0s[ic] for ic in chunks]

    yv = jnp.concatenate(ys, axis=0)
    inv_d = 1.0 / RW_HEAD_DIM
    mean = head_sum(yv) * inv_d
    yc = yv - mean
    var = head_sum(yc * yc) * inv_d
    yn = yc * lax.rsqrt(var + RW_GN_EPS) * gng_ref[...] + gnb_ref[...]
    bonus = head_sum(r * kmod * rk_ref[...]) * v
    o_ref[0] = ((yn + bonus) * gate).astype(o_ref.dtype)


def _wkv(z_rw, mu, w0, w2, a0, a2, g2, k_k, k_a, r_k, gn_g, gn_b, ts):
    b, s, _ = z_rw.shape
    assert s % ts == 0 and ts % (4 * WKV_CHUNK) == 0
    npair = RW_WIDTH // LANES
    lo_blk = (3 * RW_WIDTH) // (2 * LANES)
    row = lambda a: a.reshape(1, -1)

    def col(off):
        return pl.BlockSpec((1, ts, LANES), lambda ib, ip, it: (ib, it, off + ip))

    def vec(off):
        return pl.BlockSpec((1, LANES), lambda ib, ip, it: (0, off + ip))

    in_specs = [
        col(0), col(npair), col(2 * npair),
        pl.BlockSpec((1, ts, 2 * LANES), lambda ib, ip, it: (ib, it, lo_blk)),
        vec(0), vec(npair), vec(2 * npair),
        pl.BlockSpec((1, 2 * LANES), lambda ib, ip, it: (0, lo_blk)),
        vec(0), vec(0), vec(0), vec(0), vec(0), vec(0), vec(0),
        pl.BlockSpec((RW_DECAY_LORA, LANES), lambda ib, ip, it: (0, ip)),
        pl.BlockSpec((RW_ICLR_LORA, LANES), lambda ib, ip, it: (0, ip)),
        pl.BlockSpec((RW_GATE_LORA, LANES), lambda ib, ip, it: (0, ip)),
    ]
    scratch = [
        pltpu.VMEM((LANES, LANES), F32),
        pltpu.VMEM((1, LANES), F32), pltpu.VMEM((1, LANES), F32), pltpu.VMEM((1, LANES), F32),
        pltpu.VMEM((1, 2 * LANES), F32),
    ]
    mu2 = row(mu)
    return pl.pallas_call(
        _wkv_kernel,
        grid=(b, npair, s // ts),
        in_specs=in_specs,
        out_specs=pl.BlockSpec((1, ts, LANES), lambda ib, ip, it: (ib, it, ip)),
        out_shape=jax.ShapeDtypeStruct((b, s, RW_WIDTH), BF16),
        scratch_shapes=scratch,
        compiler_params=_cparams(("parallel", "parallel", "arbitrary")),
        name="wkv",
    )(z_rw, z_rw, z_rw, z_rw, mu2, mu2, mu2, mu2,
      row(w0), row(a0), row(k_k), row(k_a), row(r_k), row(gn_g), row(gn_b),
      w2.astype(BF16), a2.astype(BF16), g2.astype(BF16))


ATTN_ROWS = 32
assert CHUNK % ATTN_ROWS == 0


def _attn_kernel(lq1_ref, lk1_ref, lq2_ref, lk2_ref, q_ref, k_ref, v_ref, subg_ref, o_ref,
                 qs_ref, s_ref, p_ref, m_ref, alpha_ref, acc_ref, *, lambda_init, tk):
    tq = q_ref.shape[1]
    rows = 2 * tq
    qi = pl.program_id(2)
    ngroups = rows // ATTN_ROWS
    heads = (0, 1)

    lane_h0 = lax.broadcasted_iota(jnp.int32, (tq, LANES), 1) < DA_QK_DIM
    for h in heads:
        q = q_ref[0, :, h * LANES:(h + 1) * LANES]
        zero = jnp.zeros_like(q)
        qsc = (q.astype(F32) * (DA_QK_DIM ** -0.5 * math.log2(math.e))).astype(BF16)
        qs_ref[h] = jnp.concatenate([jnp.where(lane_h0, qsc, zero), jnp.where(lane_h0, zero, qsc)], axis=0)
    m_ref[...] = jnp.full_like(m_ref, -jnp.inf)
    alpha_ref[...] = jnp.zeros_like(alpha_ref)
    acc_ref[...] = jnp.zeros_like(acc_ref)
    p_ref[1] = jnp.zeros((rows, tk), BF16)

    key_chunk = lax.broadcasted_iota(jnp.int32, (ATTN_ROWS, LANES), 1) // CHUNK
    ones = jnp.ones((tk, LANES), BF16)

    def scores(h, j, width=tk):
        koff = pl.multiple_of(j * tk, tk)
        s_ref[h, :, :width] = _dot_nt(qs_ref[h], k_ref[0, pl.ds(koff, width), h * LANES:(h + 1) * LANES])

    def accumulate(h, j, width=tk):
        koff = pl.multiple_of(j * tk, tk)
        v_ext = jnp.concatenate([v_ref[0, pl.ds(koff, width), h * LANES:(h + 1) * LANES], ones[:width]], axis=1)
        alpha = alpha_ref[h]
        acc_ref[h] = jnp.concatenate([alpha, alpha], axis=1) * acc_ref[h] + _dot(p_ref[h, :, :width], v_ext)

    def softmax(h, width=tk, q_offset=None):
        def load_scores(g, t):
            r0 = g * ATTN_ROWS
            s = s_ref[h, r0:r0 + ATTN_ROWS, t * LANES:(t + 1) * LANES]
            if q_offset is None:
                return s
            q_chunk = (q_offset + r0 % tq) // CHUNK
            first_key_chunk = t * (LANES // CHUNK)
            if first_key_chunk + LANES // CHUNK - 1 <= q_chunk:
                return s
            if first_key_chunk > q_chunk:
                return None
            return jnp.where(key_chunk <= q_chunk - first_key_chunk, s, -jnp.inf)

        tiles = range(width // LANES)
        for g in range(ngroups):
            sl = slice(g * ATTN_ROWS, (g + 1) * ATTN_ROWS)
            m_prev = m_ref[h, sl, :]
            visible = [s for s in (load_scores(g, t) for t in tiles) if s is not None]
            tile_max = visible[0]
            for s in visible[1:]:
                tile_max = jnp.maximum(tile_max, s)
            m_new = jnp.maximum(m_prev, jnp.max(tile_max, axis=-1, keepdims=True))
            alpha_ref[h, sl, :] = jnp.exp2(m_prev - m_new)
            m_ref[h, sl, :] = m_new
        for g in range(ngroups):
            sl = slice(g * ATTN_ROWS, (g + 1) * ATTN_ROWS)
            m_row = m_ref[h, sl, :]
            for t in tiles:
                s = load_scores(g, t)
                p = jnp.zeros((ATTN_ROWS, LANES), BF16) if s is None else jnp.exp2(s - m_row).astype(BF16)
                p_ref[h, sl, t * LANES:(t + 1) * LANES] = p

    def step(j, width=tk, q_offset=None):
        scores(1, j, width)
        accumulate(1, jnp.maximum(j - 1, 0))
        softmax(0, width, q_offset)
        accumulate(0, j, width)
        if q_offset is None:
            scores(0, j + 1)
        softmax(1, width, q_offset)

    nfull = (qi * tq + CHUNK) // tk
    scores(0, 0)

    def full_step(j, carry):
        step(j)
        return carry

    lax.fori_loop(0, nfull, full_step, 0)
    q_per_k = tk // tq
    for v in range(q_per_k):
        @pl.when(qi % q_per_k == v)
        def _(v=v):
            width = (v + 1) * tq
            step(nfull, width, v * tq)
            accumulate(1, nfull, width)

    lam = (jnp.exp(jnp.sum(lq1_ref[...] * lk1_ref[...], axis=-1, keepdims=True))
           - jnp.exp(jnp.sum(lq2_ref[...] * lk2_ref[...], axis=-1, keepdims=True))
           + lambda_init)
    for h in heads:
        acc = acc_ref[h]
        o = acc[:, :DA_V_DIM] / acc[:, DA_V_DIM:]
        o = o[:tq, :] - lam * o[tq:, :]
        o = _rms(o, subg_ref[...], DA_SUBLN_EPS) * (1.0 - lambda_init)
        o_ref[0, :, h * LANES:(h + 1) * LANES] = o.astype(o_ref.dtype)


def _attn(z_da, lq1, lk1, lq2, lk2, sub_g, lambda_init, tq, tk):
    b, s, _ = z_da.shape
    assert tk % tq == 0 and s % tk == 0
    row = lambda a: a.reshape(1, -1)
    lam_spec = pl.BlockSpec((1, DA_QK_DIM), lambda ib, ih, iq: (0, 0))
    kern = functools.partial(_attn_kernel, lambda_init=lambda_init, tk=tk)
    rows = 2 * tq
    pair = 2 * LANES
    npairs = DA_HEADS // 2
    rep = pltpu.VMEM((2, rows, LANES), F32)
    return pl.pallas_call(
        kern,
        grid=(b, npairs, s // tq),
        in_specs=[
            lam_spec, lam_spec, lam_spec, lam_spec,
            pl.BlockSpec((1, tq, pair), lambda ib, ih, iq: (ib, iq, ih)),
            pl.BlockSpec((1, s, pair), lambda ib, ih, iq: (ib, 0, npairs + ih)),
            pl.BlockSpec((1, s, pair), lambda ib, ih, iq: (ib, 0, 2 * npairs + ih)),
            pl.BlockSpec((1, DA_V_DIM), lambda ib, ih, iq: (0, 0)),
        ],
        out_specs=pl.BlockSpec((1, tq, pair), lambda ib, ih, iq: (ib, iq, ih)),
        out_shape=jax.ShapeDtypeStruct((b, s, DA_V_WIDTH), BF16),
        scratch_shapes=[
            pltpu.VMEM((2, rows, LANES), BF16),
            pltpu.VMEM((2, rows, tk), F32),
            pltpu.VMEM((2, rows, tk), BF16),
            rep, rep,
            pltpu.VMEM((2, rows, 2 * DA_V_DIM), F32),
        ],
        compiler_params=_cparams(("parallel", "parallel", "arbitrary")),
        name="attn",
    )(row(lq1), row(lk1), row(lq2), row(lk2), z_da, z_da, z_da, row(sub_g))


def _mix_kernel(x_ref, yrw_ref, yda_ref, grw_ref, gda_ref, wrw_ref, wda_ref, wo_ref, g_ref, o_ref):
    y_rw = _dot(yrw_ref[...], wrw_ref[...])
    y_da = _dot(yda_ref[...], wda_ref[...])
    merged = grw_ref[...].astype(F32) * y_rw + gda_ref[...].astype(F32) * y_da
    mo = _dot(merged.astype(BF16), wo_ref[...])
    o_ref[...] = x_ref[...] + _rms(mo, g_ref[...], RMS_EPS)


def _mix(x2d, y_rw, y_da, z_g, w_rw, w_da, w_o, g, tm):
    t, d = x2d.shape
    full = lambda shape: pl.BlockSpec(shape, lambda i: (0, 0))
    return pl.pallas_call(
        _mix_kernel,
        grid=(t // tm,),
        in_specs=[
            pl.BlockSpec((tm, d), lambda i: (i, 0)),
            pl.BlockSpec((tm, RW_WIDTH), lambda i: (i, 0)),
            pl.BlockSpec((tm, DA_V_WIDTH), lambda i: (i, 0)),
            pl.BlockSpec((tm, d), lambda i: (i, 0)),
            pl.BlockSpec((tm, d), lambda i: (i, 1)),
            full(w_rw.shape), full(w_da.shape), full(w_o.shape), full((1, d)),
        ],
        out_specs=pl.BlockSpec((tm, d), lambda i: (i, 0)),
        out_shape=jax.ShapeDtypeStruct((t, d), F32),
        compiler_params=_cparams(("parallel",)),
        name="mix",
    )(x2d, y_rw, y_da, z_g, z_g, w_rw, w_da, w_o, g)


def _mlp_kernel(x_ref, gpre_ref, wup_ref, wdn_ref, gpost_ref, o_ref, h_ref, acc_ref):
    j = pl.program_id(1)

    @pl.when(j == 0)
    def _():
        h_ref[...] = _rms(x_ref[...], gpre_ref[...], RMS_EPS).astype(BF16)
        acc_ref[...] = jnp.zeros_like(acc_ref)

    up = _dot(h_ref[...], wup_ref[...])
    act = jnp.square(jnp.maximum(up, 0.0))
    acc_ref[...] += _dot(act.astype(BF16), wdn_ref[...])

    @pl.when(j == pl.num_programs(1) - 1)
    def _():
        o_ref[...] = x_ref[...] + _rms(acc_ref[...], gpost_ref[...], RMS_EPS)


def _mlp(x2d, g_pre, w_up, w_dn, g_post, tm, tf):
    t, d = x2d.shape
    ff = w_up.shape[1]
    return pl.pallas_call(
        _mlp_kernel,
        grid=(t // tm, ff // tf),
        in_specs=[
            pl.BlockSpec((tm, d), lambda i, j: (i, 0)),
            pl.BlockSpec((1, d), lambda i, j: (0, 0)),
            pl.BlockSpec((d, tf), lambda i, j: (0, j)),
            pl.BlockSpec((tf, d), lambda i, j: (j, 0)),
            pl.BlockSpec((1, d), lambda i, j: (0, 0)),
        ],
        out_specs=pl.BlockSpec((tm, d), lambda i, j: (i, 0)),
        out_shape=jax.ShapeDtypeStruct((t, d), F32),
        scratch_shapes=[pltpu.VMEM((tm, d), BF16), pltpu.VMEM((tm, d), F32)],
        compiler_params=_cparams(("parallel", "arbitrary")),
        name="mlp",
    )(x2d, g_pre, w_up, w_dn, g_post)


def _tile(n, want):
    t = min(n, want)
    while n % t:
        t //= 2
    return t


class _Tiles(NamedTuple):
    inproj_rows: int
    wkv_tokens: int
    attn_q: int
    attn_k: int
    mix_rows: int
    mlp_rows: int
    mlp_ff: int


def _plan(batch, seq, d_ff):
    tokens = batch * seq
    attn_q = _tile(seq, 512)
    return _Tiles(
        inproj_rows=_tile(tokens, 512),
        wkv_tokens=_tile(seq, 16 * WKV_CHUNK),
        attn_q=attn_q,
        attn_k=_tile(seq, 2 * attn_q),
        mix_rows=_tile(tokens, 1024),
        mlp_rows=_tile(tokens, 1024),
        mlp_ff=_tile(d_ff, 1024),
    )


def kernel(x, pre_mix_g, w_in, rw_shift_mu, rw_w0, rw_w2, rw_a0, rw_a2, rw_g2, rw_k_k, rw_k_a, rw_r_k, rw_gn_g, rw_gn_b, w_rw_out, da_lam_q1, da_lam_k1, da_lam_q2, da_lam_k2, da_sub_g, w_da_out, w_o, post_mix_g, pre_mlp_g, w_up, w_down, post_mlp_g):
    b, s, d = x.shape
    depth = w_in.shape[0]
    t = b * s
    tiles = _plan(b, s, w_up.shape[2])
    for l in range(depth):
        lambda_init = 0.8 - 0.6 * math.exp(-0.3 * l)
        x2d = x.reshape(t, d)
        z_rw, z_da, z_g = _inproj(x2d, pre_mix_g[l].reshape(1, d), w_in[l].astype(BF16), tiles.inproj_rows)

        y_rw = _wkv(z_rw.reshape(b, s, RW_IN), rw_shift_mu[l], rw_w0[l], rw_w2[l], rw_a0[l],
                    rw_a2[l], rw_g2[l], rw_k_k[l], rw_k_a[l], rw_r_k[l], rw_gn_g[l], rw_gn_b[l],
                    tiles.wkv_tokens)
        y_da = _attn(z_da.reshape(b, s, DA_IN), da_lam_q1[l], da_lam_k1[l], da_lam_q2[l],
                     da_lam_k2[l], da_sub_g[l], lambda_init, tiles.attn_q, tiles.attn_k)

        x1 = _mix(x2d, y_rw.reshape(t, RW_WIDTH), y_da.reshape(t, DA_V_WIDTH), z_g,
                  w_rw_out[l].astype(BF16), w_da_out[l].astype(BF16), w_o[l].astype(BF16),
                  post_mix_g[l].reshape(1, d), tiles.mix_rows)
        x2 = _mlp(x1, pre_mlp_g[l].reshape(1, d), w_up[l].astype(BF16), w_down[l].astype(BF16),
                  post_mlp_g[l].reshape(1, d), tiles.mlp_rows, tiles.mlp_ff)
        x = x2.reshape(b, s, d)
    return x
```

```python
import functools
import math
from typing import NamedTuple

import jax
import jax.numpy as jnp
from jax import lax
from jax.experimental import pallas as pl
from jax.experimental.pallas import tpu as pltpu

F32 = jnp.float32
BF16 = jnp.bfloat16

RMS_EPS = 1e-6
CHUNK = 64
RW_HEADS = 8
RW_HEAD_DIM = 64
RW_WIDTH = RW_HEADS * RW_HEAD_DIM
RW_DECAY_LORA = 64
RW_ICLR_LORA = 64
RW_GATE_LORA = 128
RW_GN_EPS = 1e-5 * RW_HEAD_DIM
RW_IN = 3 * RW_WIDTH + RW_DECAY_LORA + RW_ICLR_LORA + RW_GATE_LORA
DA_HEADS = 4
DA_QK_DIM = 64
DA_V_DIM = 2 * DA_QK_DIM
DA_QK_WIDTH = DA_HEADS * 2 * DA_QK_DIM
DA_V_WIDTH = DA_HEADS * DA_V_DIM
DA_IN = 2 * DA_QK_WIDTH + DA_V_WIDTH
DA_SUBLN_EPS = 1e-5

LANES = 128
WKV_CHUNK = 64
VMEM_LIMIT = 56 * 1024 * 1024


def _cparams(semantics):
    return pltpu.CompilerParams(dimension_semantics=semantics, vmem_limit_bytes=VMEM_LIMIT)


def _rms(x, g, eps):
    return x * lax.rsqrt(jnp.mean(x * x, axis=-1, keepdims=True) + eps) * g


def _dot(a, b):
    return jnp.dot(a, b, preferred_element_type=F32)


def _dot_nt(a, b):
    return lax.dot_general(a, b, (((1,), (1,)), ((), ())), preferred_element_type=F32)


def _dot_tn(a, b):
    return lax.dot_general(a, b, (((0,), (0,)), ((), ())), preferred_element_type=F32)


def _inproj_kernel(x_ref, g_ref, w_ref, orw_ref, oda_ref, og_ref):
    h = _rms(x_ref[...], g_ref[...], RMS_EPS).astype(BF16)
    o1, o2 = RW_IN, RW_IN + DA_IN
    orw_ref[...] = _dot(h, w_ref[:, :o1])
    oda_ref[...] = _dot(h, w_ref[:, o1:o2]).astype(oda_ref.dtype)
    og_ref[...] = _sigmoid(_dot(h, w_ref[:, o2:])).astype(og_ref.dtype)


def _inproj(x2d, g, w_bf16, tm):
    t, d = x2d.shape
    n = w_bf16.shape[1]
    n_gate = n - RW_IN - DA_IN
    return pl.pallas_call(
        _inproj_kernel,
        grid=(t // tm,),
        in_specs=[
            pl.BlockSpec((tm, d), lambda i: (i, 0)),
            pl.BlockSpec((1, d), lambda i: (0, 0)),
            pl.BlockSpec((d, n), lambda i: (0, 0)),
        ],
        out_specs=[
            pl.BlockSpec((tm, RW_IN), lambda i: (i, 0)),
            pl.BlockSpec((tm, DA_IN), lambda i: (i, 0)),
            pl.BlockSpec((tm, n_gate), lambda i: (i, 0)),
        ],
        out_shape=[
            jax.ShapeDtypeStruct((t, RW_IN), F32),
            jax.ShapeDtypeStruct((t, DA_IN), BF16),
            jax.ShapeDtypeStruct((t, n_gate), BF16),
        ],
        compiler_params=_cparams(("parallel",)),
        name="inproj",
    )(x2d, g, w_bf16)


def _sigmoid(x):
    return 1.0 / (1.0 + jnp.exp(-x))


def _split_bf16(x, n):
    parts = []
    for _ in range(n - 1):
        hi = x.astype(BF16)
        parts.append(hi)
        x = x - hi.astype(F32)
    parts.append(x.astype(BF16))
    return parts


def _wkv_kernel(r_ref, k_ref, v_ref, lo_ref, mur_ref, muk_ref, muv_ref, mulo_ref,
                w0_ref, a0_ref, kk_ref, ka_ref, rk_ref, gng_ref, gnb_ref,
                w2_ref, a2_ref, g2_ref, o_ref,
                p_ref, cr_ref, ck_ref, cv_ref, clo_ref):
    ts = r_ref.shape[1]
    nc = ts // WKV_CHUNK
    c = WKV_CHUNK

    @pl.when(pl.program_id(2) == 0)
    def _():
        p_ref[...] = jnp.zeros_like(p_ref)
        cr_ref[...] = jnp.zeros_like(cr_ref)
        ck_ref[...] = jnp.zeros_like(ck_ref)
        cv_ref[...] = jnp.zeros_like(cv_ref)
        clo_ref[...] = jnp.zeros_like(clo_ref)

    def shifted(z, carry_ref, mu):
        prev = pltpu.roll(z, 1, axis=0)
        row = lax.broadcasted_iota(jnp.int32, z.shape, 0)
        prev = jnp.where(row == 0, carry_ref[...], prev)
        carry_ref[...] = z[ts - 1:ts, :]
        return z + (prev - z) * mu

    r = shifted(r_ref[0], cr_ref, mur_ref[...])
    k = shifted(k_ref[0], ck_ref, muk_ref[...])
    v = shifted(v_ref[0], cv_ref, muv_ref[...])
    lo = shifted(lo_ref[0], clo_ref, mulo_ref[...])
    xwa = lo[:, :LANES]
    xg = lo[:, LANES:]

    zeros64 = jnp.zeros((RW_DECAY_LORA, LANES), BF16)
    w2p = jnp.concatenate([w2_ref[...], zeros64], axis=0)
    a2p = jnp.concatenate([zeros64, a2_ref[...]], axis=0)
    wl = _dot(jnp.tanh(xwa).astype(BF16), w2p)
    al = _dot(xwa.astype(BF16), a2p)
    gate = _dot(_sigmoid(xg).astype(BF16), g2_ref[...])

    y = -(w0_ref[...] + wl)
    softplus = jnp.maximum(y, 0.0) + jnp.log(1.0 + jnp.exp(-jnp.abs(y)))
    lw = -jnp.exp(-softplus - 0.5)
    a = _sigmoid(a0_ref[...] + al)

    lane = lax.broadcasted_iota(jnp.int32, (LANES, LANES), 1)
    rowi = lax.broadcasted_iota(jnp.int32, (LANES, LANES), 0)
    same_head = (lane // RW_HEAD_DIM) == (rowi // RW_HEAD_DIM)
    head_ones = jnp.where(same_head, 1.0, 0.0).astype(BF16)

    def head_sum(x):
        hi, lo = _split_bf16(x, 2)
        return _dot(hi, head_ones) + _dot(lo, head_ones)

    kk = k * kk_ref[...]
    kk = kk / jnp.maximum(jnp.sqrt(head_sum(kk * kk)), 1e-12)
    kmod = k * (1.0 + (a - 1.0) * ka_ref[...])

    bb = kk * a

    ri = lax.broadcasted_iota(jnp.int32, (c, c), 0)
    ci = lax.broadcasted_iota(jnp.int32, (c, c), 1)
    tri_incl = jnp.where(ci <= ri, 1.0, 0.0).astype(BF16)
    lane_h0 = lax.broadcasted_iota(jnp.int32, (c, LANES), 1) < RW_HEAD_DIM
    strict = lane < rowi
    incl = lane <= rowi
    eye = lane == rowi
    eye_f = jnp.where(eye, 1.0, 0.0)

    def bd(x):
        zero = jnp.zeros_like(x)
        return jnp.concatenate([jnp.where(lane_h0, x, zero), jnp.where(lane_h0, zero, x)], axis=0)

    def level_mask(m):
        return ((rowi // (2 * m)) == (lane // (2 * m))) & ((rowi % (2 * m)) >= m) & ((lane % (2 * m)) < m)

    level_masks = {m: level_mask(m) for m in (1, 2, 4, 8, 16, 32)}

    lw_parts = jnp.concatenate(_split_bf16(lw, 3), axis=1)
    cums = []
    for ic in range(nc):
        cum3 = _dot(tri_incl, lw_parts[ic * c:(ic + 1) * c, :])
        cums.append(cum3[:, :LANES] + cum3[:, LANES:2 * LANES] + cum3[:, 2 * LANES:])
    cum = jnp.concatenate(cums, axis=0)
    einv = jnp.exp(-cum)
    a_til = kk * jnp.exp(cum - lw)
    r_til = r * jnp.exp(cum)
    k_til = kmod * einv
    b_til = bb * einv

    chunks = range(nc)
    rows_of = lambda x, ic: x[ic * c:(ic + 1) * c]
    w_end = [jnp.exp(cums[ic][c - 1:c, :]) for ic in chunks]
    a_t = [bd(rows_of(a_til, ic).astype(BF16)) for ic in chunks]
    r_t = [bd(rows_of(r_til, ic)) for ic in chunks]
    k_t = [bd(rows_of(k_til, ic).astype(BF16)) for ic in chunks]
    b_t = [bd(rows_of(b_til, ic).astype(BF16)) for ic in chunks]
    k_e = [bd((rows_of(k_til, ic) * w_end[ic]).astype(BF16)) for ic in chunks]
    b_e = [bd((rows_of(b_til, ic) * w_end[ic]).astype(BF16)) for ic in chunks]
    v_b = [bd(rows_of(v, ic).astype(BF16)) for ic in chunks]

    sc = [_dot_nt(jnp.concatenate([a_t[ic], r_t[ic].astype(BF16)], axis=0),
                  jnp.concatenate([k_t[ic], b_t[ic]], axis=0)) for ic in chunks]
    l_ab = [jnp.where(strict, s_[:LANES, LANES:], 0.0) for s_ in sc]
    a_ak = [jnp.where(strict, s_[:LANES, :LANES], 0.0).astype(BF16) for s_ in sc]
    a_rk = [jnp.where(incl, s_[LANES:, :LANES], 0.0).astype(BF16) for s_ in sc]
    a_rb = [jnp.where(incl, s_[LANES:, LANES:], 0.0).astype(BF16) for s_ in sc]

    x = [eye_f - jnp.where(level_masks[1], l_, 0.0) for l_ in l_ab]
    for m in (2, 4, 8, 16, 32):
        xb = [x_.astype(BF16) for x_ in x]
        xc = [_dot(xb[ic], jnp.where(level_masks[m], l_ab[ic], 0.0).astype(BF16)) for ic in chunks]
        x = [x[ic] - _dot(xc[ic].astype(BF16), xb[ic]) for ic in chunks]

    akv = [_dot(a_ak[ic], v_b[ic]) for ic in chunks]
    rkv = [_dot(a_rk[ic], v_b[ic]) for ic in chunks]
    ktv = [_dot_tn(k_e[ic], v_b[ic]) for ic in chunks]
    ua = [_dot(x[ic].astype(BF16), jnp.concatenate([akv[ic].astype(BF16), a_t[ic]], axis=1)).astype(BF16)
          for ic in chunks]
    rb_ua = [_dot(a_rb[ic], ua[ic]) for ic in chunks]
    bt_ua = [_dot_tn(b_e[ic], ua[ic]) for ic in chunks]
    r_bars = [(r_t[ic] - rb_ua[ic][:, LANES:]).astype(BF16) for ic in chunks]
    y0 = [rkv[ic] - rb_ua[ic][:, :LANES] for ic in chunks]
    y0s = [y_[:c, :] + y_[c:, :] for y_ in y0]
    g_mats = [(jnp.where(eye, w_end[ic], 0.0) - bt_ua[ic][:, LANES:]).astype(BF16) for ic in chunks]
    h_mats = [ktv[ic] - bt_ua[ic][:, :LANES] for ic in chunks]

    def compose(g_lo, h_lo, g_hi, h_hi):
        gh = _dot(g_hi, jnp.concatenate([g_lo, h_lo.astype(BF16)], axis=1))
        return gh[:, :LANES].astype(BF16), gh[:, LANES:] + h_hi

    pairs = range(nc // 2)
    quads = range(nc // 4)
    gh2 = [compose(g_mats[2 * k], h_mats[2 * k], g_mats[2 * k + 1], h_mats[2 * k + 1]) for k in pairs]
    gh4 = [compose(*gh2[2 * q], *gh2[2 * q + 1]) for q in quads]
    p = p_ref[...]
    p_quad = []
    for q in quads:
        pb = p.astype(BF16)
        p_quad.append(pb)
        p = _dot(gh4[q][0], pb) + gh4[q][1]
    p_ref[...] = p
    p_mid = [(_dot(gh2[2 * q][0], p_quad[q]) + gh2[2 * q][1]).astype(BF16) for q in quads]
    p_even = [p_quad[k // 2] if k % 2 == 0 else p_mid[k // 2] for k in pairs]
    p_odd = [(_dot(g_mats[2 * k], p_even[k]) + h_mats[2 * k]).astype(BF16) for k in pairs]
    states = [p_even[ic // 2] if ic % 2 == 0 else p_odd[ic // 2] for ic in chunks]
    y_bd = [_dot(r_bars[ic], states[ic]) for ic in chunks]
    ys = [y_bd[ic][:c, :] + y_bd[ic][c:, :] + y0s[ic] for ic in chunks]

    yv = jnp.concatenate(ys, axis=0)
    inv_d = 1.0 / RW_HEAD_DIM
    mean = head_sum(yv) * inv_d
    yc = yv - mean
    var = head_sum(yc * yc) * inv_d
    yn = yc * lax.rsqrt(var + RW_GN_EPS) * gng_ref[...] + gnb_ref[...]
    bonus = head_sum(r * kmod * rk_ref[...]) * v
    o_ref[0] = ((yn + bonus) * gate).astype(o_ref.dtype)


def _wkv(z_rw, mu, w0, w2, a0, a2, g2, k_k, k_a, r_k, gn_g, gn_b, ts):
    b, s, _ = z_rw.shape
    assert s % ts == 0 and ts % (4 * WKV_CHUNK) == 0
    npair = RW_WIDTH // LANES
    lo_blk = (3 * RW_WIDTH) // (2 * LANES)
    row = lambda a: a.reshape(1, -1)

    def col(off):
        return pl.BlockSpec((1, ts, LANES), lambda ib, ip, it: (ib, it, off + ip))

    def vec(off):
        return pl.BlockSpec((1, LANES), lambda ib, ip, it: (0, off + ip))

    in_specs = [
        col(0), col(npair), col(2 * npair),
        pl.BlockSpec((1, ts, 2 * LANES), lambda ib, ip, it: (ib, it, lo_blk)),
        vec(0), vec(npair), vec(2 * npair),
        pl.BlockSpec((1, 2 * LANES), lambda ib, ip, it: (0, lo_blk)),
        vec(0), vec(0), vec(0), vec(0), vec(0), vec(0), vec(0),
        pl.BlockSpec((RW_DECAY_LORA, LANES), lambda ib, ip, it: (0, ip)),
        pl.BlockSpec((RW_ICLR_LORA, LANES), lambda ib, ip, it: (0, ip)),
        pl.BlockSpec((RW_GATE_LORA, LANES), lambda ib, ip, it: (0, ip)),
    ]
    scratch = [
        pltpu.VMEM((LANES, LANES), F32),
        pltpu.VMEM((1, LANES), F32), pltpu.VMEM((1, LANES), F32), pltpu.VMEM((1, LANES), F32),
        pltpu.VMEM((1, 2 * LANES), F32),
    ]
    mu2 = row(mu)
    return pl.pallas_call(
        _wkv_kernel,
        grid=(b, npair, s // ts),
        in_specs=in_specs,
        out_specs=pl.BlockSpec((1, ts, LANES), lambda ib, ip, it: (ib, it, ip)),
        out_shape=jax.ShapeDtypeStruct((b, s, RW_WIDTH), BF16),
        scratch_shapes=scratch,
        compiler_params=_cparams(("parallel", "parallel", "arbitrary")),
        name="wkv",
    )(z_rw, z_rw, z_rw, z_rw, mu2, mu2, mu2, mu2,
      row(w0), row(a0), row(k_k), row(k_a), row(r_k), row(gn_g), row(gn_b),
      w2.astype(BF16), a2.astype(BF16), g2.astype(BF16))


ATTN_ROWS = 32
assert CHUNK % ATTN_ROWS == 0


def _attn_kernel(lq1_ref, lk1_ref, lq2_ref, lk2_ref, q_ref, k_ref, v_ref, subg_ref, o_ref,
                 qs_ref, s_ref, p_ref, m_ref, alpha_ref, acc_ref, *, lambda_init, tk):
    tq = q_ref.shape[1]
    rows = 2 * tq
    qi = pl.program_id(2)
    ngroups = rows // ATTN_ROWS
    heads = (0, 1)

    lane_h0 = lax.broadcasted_iota(jnp.int32, (tq, LANES), 1) < DA_QK_DIM
    for h in heads:
        q = q_ref[0, :, h * LANES:(h + 1) * LANES]
        zero = jnp.zeros_like(q)
        qsc = (q.astype(F32) * (DA_QK_DIM ** -0.5 * math.log2(math.e))).astype(BF16)
        qs_ref[h] = jnp.concatenate([jnp.where(lane_h0, qsc, zero), jnp.where(lane_h0, zero, qsc)], axis=0)
    m_ref[...] = jnp.full_like(m_ref, -jnp.inf)
    alpha_ref[...] = jnp.zeros_like(alpha_ref)
    acc_ref[...] = jnp.zeros_like(acc_ref)
    p_ref[1] = jnp.zeros((rows, tk), BF16)

    key_chunk = lax.broadcasted_iota(jnp.int32, (ATTN_ROWS, LANES), 1) // CHUNK
    ones = jnp.ones((tk, LANES), BF16)

    def scores(h, j, width=tk):
        koff = pl.multiple_of(j * tk, tk)
        s_ref[h, :, :width] = _dot_nt(qs_ref[h], k_ref[0, pl.ds(koff, width), h * LANES:(h + 1) * LANES])

    def accumulate(h, j, width=tk):
        koff = pl.multiple_of(j * tk, tk)
        v_ext = jnp.concatenate([v_ref[0, pl.ds(koff, width), h * LANES:(h + 1) * LANES], ones[:width]], axis=1)
        alpha = alpha_ref[h]
        acc_ref[h] = jnp.concatenate([alpha, alpha], axis=1) * acc_ref[h] + _dot(p_ref[h, :, :width], v_ext)

    def softmax(h, width=tk, q_offset=None):
        def load_scores(g, t):
            r0 = g * ATTN_ROWS
            s = s_ref[h, r0:r0 + ATTN_ROWS, t * LANES:(t + 1) * LANES]
            if q_offset is None:
                return s
            q_chunk = (q_offset + r0 % tq) // CHUNK
            first_key_chunk = t * (LANES // CHUNK)
            if first_key_chunk + LANES // CHUNK - 1 <= q_chunk:
                return s
            if first_key_chunk > q_chunk:
                return None
            return jnp.where(key_chunk <= q_chunk - first_key_chunk, s, -jnp.inf)

        tiles = range(width // LANES)
        for g in range(ngroups):
            sl = slice(g * ATTN_ROWS, (g + 1) * ATTN_ROWS)
            m_prev = m_ref[h, sl, :]
            visible = [s for s in (load_scores(g, t) for t in tiles) if s is not None]
            tile_max = visible[0]
            for s in visible[1:]:
                tile_max = jnp.maximum(tile_max, s)
            m_new = jnp.maximum(m_prev, jnp.max(tile_max, axis=-1, keepdims=True))
            alpha_ref[h, sl, :] = jnp.exp2(m_prev - m_new)
            m_ref[h, sl, :] = m_new
        for g in range(ngroups):
            sl = slice(g * ATTN_ROWS, (g + 1) * ATTN_ROWS)
            m_row = m_ref[h, sl, :]
            for t in tiles:
                s = load_scores(g, t)
                p = jnp.zeros((ATTN_ROWS, LANES), BF16) if s is None else jnp.exp2(s - m_row).astype(BF16)
                p_ref[h, sl, t * LANES:(t + 1) * LANES] = p

    def step(j, width=tk, q_offset=None):
        scores(1, j, width)
        accumulate(1, jnp.maximum(j - 1, 0))
        softmax(0, width, q_offset)
        accumulate(0, j, width)
        if q_offset is None:
            scores(0, j + 1)
        softmax(1, width, q_offset)

    nfull = (qi * tq + CHUNK) // tk
    scores(0, 0)

    def full_step(j, carry):
        step(j)
        return carry

    lax.fori_loop(0, nfull, full_step, 0)
    q_per_k = tk // tq
    for v in range(q_per_k):
        @pl.when(qi % q_per_k == v)
        def _(v=v):
            width = (v + 1) * tq
            step(nfull, width, v * tq)
            accumulate(1, nfull, width)

    lam = (jnp.exp(jnp.sum(lq1_ref[...] * lk1_ref[...], axis=-1, keepdims=True))
           - jnp.exp(jnp.sum(lq2_ref[...] * lk2_ref[...], axis=-1, keepdims=True))
           + lambda_init)
    for h in heads:
        acc = acc_ref[h]
        o = acc[:, :DA_V_DIM] / acc[:, DA_V_DIM:]
        o = o[:tq, :] - lam * o[tq:, :]
        o = _rms(o, subg_ref[...], DA_SUBLN_EPS) * (1.0 - lambda_init)
        o_ref[0, :, h * LANES:(h + 1) * LANES] = o.astype(o_ref.dtype)


def _attn(z_da, lq1, lk1, lq2, lk2, sub_g, lambda_init, tq, tk):
    b, s, _ = z_da.shape
    assert tk % tq == 0 and s % tk == 0
    row = lambda a: a.reshape(1, -1)
    lam_spec = pl.BlockSpec((1, DA_QK_DIM), lambda ib, ih, iq: (0, 0))
    kern = functools.partial(_attn_kernel, lambda_init=lambda_init, tk=tk)
    rows = 2 * tq
    pair = 2 * LANES
    npairs = DA_HEADS // 2
    rep = pltpu.VMEM((2, rows, LANES), F32)
    return pl.pallas_call(
        kern,
        grid=(b, npairs, s // tq),
        in_specs=[
            lam_spec, lam_spec, lam_spec, lam_spec,
            pl.BlockSpec((1, tq, pair), lambda ib, ih, iq: (ib, iq, ih)),
            pl.BlockSpec((1, s, pair), lambda ib, ih, iq: (ib, 0, npairs + ih)),
            pl.BlockSpec((1, s, pair), lambda ib, ih, iq: (ib, 0, 2 * npairs + ih)),
            pl.BlockSpec((1, DA_V_DIM), lambda ib, ih, iq: (0, 0)),
        ],
        out_specs=pl.BlockSpec((1, tq, pair), lambda ib, ih, iq: (ib, iq, ih)),
        out_shape=jax.ShapeDtypeStruct((b, s, DA_V_WIDTH), BF16),
        scratch_shapes=[
            pltpu.VMEM((2, rows, LANES), BF16),
            pltpu.VMEM((2, rows, tk), F32),
            pltpu.VMEM((2, rows, tk), BF16),
            rep, rep,
            pltpu.VMEM((2, rows, 2 * DA_V_DIM), F32),
        ],
        compiler_params=_cparams(("parallel", "parallel", "arbitrary")),
        name="attn",
    )(row(lq1), row(lk1), row(lq2), row(lk2), z_da, z_da, z_da, row(sub_g))


def _mix_kernel(x_ref, yrw_ref, yda_ref, grw_ref, gda_ref, wrw_ref, wda_ref, wo_ref, g_ref, o_ref):
    y_rw = _dot(yrw_ref[...], wrw_ref[...])
    y_da = _dot(yda_ref[...], wda_ref[...])
    merged = grw_ref[...].astype(F32) * y_rw + gda_ref[...].astype(F32) * y_da
    mo = _dot(merged.astype(BF16), wo_ref[...])
    o_ref[...] = x_ref[...] + _rms(mo, g_ref[...], RMS_EPS)


def _mix(x2d, y_rw, y_da, z_g, w_rw, w_da, w_o, g, tm):
    t, d = x2d.shape
    full = lambda shape: pl.BlockSpec(shape, lambda i: (0, 0))
    return pl.pallas_call(
        _mix_kernel,
        grid=(t // tm,),
        in_specs=[
            pl.BlockSpec((tm, d), lambda i: (i, 0)),
            pl.BlockSpec((tm, RW_WIDTH), lambda i: (i, 0)),
            pl.BlockSpec((tm, DA_V_WIDTH), lambda i: (i, 0)),
            pl.BlockSpec((tm, d), lambda i: (i, 0)),
            pl.BlockSpec((tm, d), lambda i: (i, 1)),
            full(w_rw.shape), full(w_da.shape), full(w_o.shape), full((1, d)),
        ],
        out_specs=pl.BlockSpec((tm, d), lambda i: (i, 0)),
        out_shape=jax.ShapeDtypeStruct((t, d), F32),
        compiler_params=_cparams(("parallel",)),
        name="mix",
    )(x2d, y_rw, y_da, z_g, z_g, w_rw, w_da, w_o, g)


def _mlp_kernel(x_ref, gpre_ref, wup_ref, wdn_ref, gpost_ref, o_ref, h_ref, acc_ref):
    j = pl.program_id(1)

    @pl.when(j == 0)
    def _():
        h_ref[...] = _rms(x_ref[...], gpre_ref[...], RMS_EPS).astype(BF16)
        acc_ref[...] = jnp.zeros_like(acc_ref)

    up = _dot(h_ref[...], wup_ref[...])
    act = jnp.square(jnp.maximum(up, 0.0))
    acc_ref[...] += _dot(act.astype(BF16), wdn_ref[...])

    @pl.when(j == pl.num_programs(1) - 1)
    def _():
        o_ref[...] = x_ref[...] + _rms(acc_ref[...], gpost_ref[...], RMS_EPS)


def _mlp(x2d, g_pre, w_up, w_dn, g_post, tm, tf):
    t, d = x2d.shape
    ff = w_up.shape[1]
    return pl.pallas_call(
        _mlp_kernel,
        grid=(t // tm, ff // tf),
        in_specs=[
            pl.BlockSpec((tm, d), lambda i, j: (i, 0)),
            pl.BlockSpec((1, d), lambda i, j: (0, 0)),
            pl.BlockSpec((d, tf), lambda i, j: (0, j)),
            pl.BlockSpec((tf, d), lambda i, j: (j, 0)),
            pl.BlockSpec((1, d), lambda i, j: (0, 0)),
        ],
        out_specs=pl.BlockSpec((tm, d), lambda i, j: (i, 0)),
        out_shape=jax.ShapeDtypeStruct((t, d), F32),
        scratch_shapes=[pltpu.VMEM((tm, d), BF16), pltpu.VMEM((tm, d), F32)],
        compiler_params=_cparams(("parallel", "arbitrary")),
        name="mlp",
    )(x2d, g_pre, w_up, w_dn, g_post)


def _tile(n, want):
    t = min(n, want)
    while n % t:
        t //= 2
    return t


class _Tiles(NamedTuple):
    inproj_rows: int
    wkv_tokens: int
    attn_q: int
    attn_k: int
    mix_rows: int
    mlp_rows: int
    mlp_ff: int


def _plan(batch, seq, d_ff):
    tokens = batch * seq
    attn_q = _tile(seq, 512)
    return _Tiles(
        inproj_rows=_tile(tokens, 512),
        wkv_tokens=_tile(seq, 16 * WKV_CHUNK),
        attn_q=attn_q,
        attn_k=_tile(seq, 2 * attn_q),
        mix_rows=_tile(tokens, 1024),
        mlp_rows=_tile(tokens, 1024),
        mlp_ff=_tile(d_ff, 2048),
    )


def kernel(x, pre_mix_g, w_in, rw_shift_mu, rw_w0, rw_w2, rw_a0, rw_a2, rw_g2, rw_k_k, rw_k_a, rw_r_k, rw_gn_g, rw_gn_b, w_rw_out, da_lam_q1, da_lam_k1, da_lam_q2, da_lam_k2, da_sub_g, w_da_out, w_o, post_mix_g, pre_mlp_g, w_up, w_down, post_mlp_g):
    b, s, d = x.shape
    depth = w_in.shape[0]
    t = b * s
    tiles = _plan(b, s, w_up.shape[2])
    for l in range(depth):
        lambda_init = 0.8 - 0.6 * math.exp(-0.3 * l)
        x2d = x.reshape(t, d)
        z_rw, z_da, z_g = _inproj(x2d, pre_mix_g[l].reshape(1, d), w_in[l].astype(BF16), tiles.inproj_rows)

        y_rw = _wkv(z_rw.reshape(b, s, RW_IN), rw_shift_mu[l], rw_w0[l], rw_w2[l], rw_a0[l],
                    rw_a2[l], rw_g2[l], rw_k_k[l], rw_k_a[l], rw_r_k[l], rw_gn_g[l], rw_gn_b[l],
                    tiles.wkv_tokens)
        y_da = _attn(z_da.reshape(b, s, DA_IN), da_lam_q1[l], da_lam_k1[l], da_lam_q2[l],
                     da_lam_k2[l], da_sub_g[l], lambda_init, tiles.attn_q, tiles.attn_k)

        x1 = _mix(x2d, y_rw.reshape(t, RW_WIDTH), y_da.reshape(t, DA_V_WIDTH), z_g,
                  w_rw_out[l].astype(BF16), w_da_out[l].astype(BF16), w_o[l].astype(BF16),
                  post_mix_g[l].reshape(1, d), tiles.mix_rows)
        x2 = _mlp(x1, pre_mlp_g[l].reshape(1, d), w_up[l].astype(BF16), w_down[l].astype(BF16),
                  post_mlp_g[l].reshape(1, d), tiles.mlp_rows, tiles.mlp_ff)
        x = x2.reshape(b, s, d)
    return x
```
